```python
import jax, jax.numpy as jnp
from jax import lax
import numpy as np

D_MODEL = 2048
BATCH = 2
SEQ = 8192
DEPTH = 1

CHUNK = 64
HEAD_DIM = 64
D_SSD = D_MODEL
D_RWKV = D_MODEL
D_MIX = D_SSD + D_RWKV
SSD_HEADS = D_SSD // HEAD_DIM
SSD_GROUPS = 4
SSD_HPG = SSD_HEADS // SSD_GROUPS
SSD_STATE = 128
CONV_WIDTH = 4
D_XBC = D_SSD + 2 * SSD_GROUPS * SSD_STATE
RWKV_HEADS = D_RWKV // HEAD_DIM
DECAY_LORA = 96
AAA_LORA = 96
GATE_LORA = 256
D_RWKV_IN = 3 * D_RWKV + DECAY_LORA + AAA_LORA + GATE_LORA
D_IN = D_SSD + D_XBC + SSD_HEADS + D_RWKV_IN
D_FF = -(-8 * D_MODEL // (3 * 256)) * 256
RMS_EPS = 1e-6
GATED_NORM_EPS = 1e-5
GN_EPS = 64e-5

kernel_name = "hymba_style_ssd_rwkv7_hybrid_block"


def rms_norm(x, g, eps=RMS_EPS):
    xf = x.astype(jnp.float32)
    y = xf * lax.rsqrt(jnp.mean(xf * xf, axis=-1, keepdims=True) + eps)
    return (y * g.astype(jnp.float32)).astype(x.dtype)


def causal_depthwise_conv(u, w, b):
    c = u.shape[-1]
    y = lax.conv_general_dilated(u, w[:, None, :].astype(u.dtype), window_strides=(1,),
                                 padding=((w.shape[0] - 1, 0),),
                                 dimension_numbers=('NWC', 'WIO', 'NWC'),
                                 feature_group_count=c)
    return y + b.astype(u.dtype)


def segsum(a):
    t = a.shape[-1]
    rep = jnp.broadcast_to(a[..., None], a.shape + (t,))
    rep = jnp.where(jnp.tril(jnp.ones((t, t), bool), -1), rep, 0.0)
    cs = jnp.cumsum(rep, axis=-2)
    return jnp.where(jnp.tril(jnp.ones((t, t), bool)), cs, -jnp.inf)


def ssd_chunked(xh, dt, A, Bm, Cm):
    b, s, g, e, p = xh.shape
    n = Bm.shape[-1]
    c = s // CHUNK
    X = (xh * dt[..., None]).reshape(b, c, CHUNK, g, e, p)
    Adt = (dt * A).reshape(b, c, CHUNK, g, e).transpose(0, 3, 4, 1, 2)
    Bc = Bm.reshape(b, c, CHUNK, g, n)
    Cc = Cm.reshape(b, c, CHUNK, g, n)
    A_cs = jnp.cumsum(Adt, axis=-1)
    L = jnp.exp(segsum(Adt))
    CB = jnp.einsum('bclgn,bcsgn->bcgls', Cc, Bc)
    y_diag = jnp.einsum('bcgls,bgecls,bcsgep->bclgep', CB, L, X)
    decay_states = jnp.exp(A_cs[..., -1:] - A_cs)
    states = jnp.einsum('bclgn,bgecl,bclgep->bcgepn', Bc, decay_states, X)
    chunk_decay = jnp.exp(A_cs[..., -1])

    def step(h, inp):
        st, dec = inp
        return h * dec[..., None, None] + st, h

    h0 = jnp.zeros((b, g, e, p, n), X.dtype)
    _, h_in = lax.scan(step, h0, (states.transpose(1, 0, 2, 3, 4, 5), chunk_decay.transpose(3, 0, 1, 2)))
    h_in = h_in.transpose(1, 0, 2, 3, 4, 5)
    y_off = jnp.einsum('bclgn,bcgepn,bgecl->bclgep', Cc, h_in, jnp.exp(A_cs))
    return (y_diag + y_off).reshape(b, s, g, e, p)


def ssd_mixer(z, xbc, dt_raw, conv_w, conv_b, dt_bias, A_log, D_skip, norm_g):
    b, s, _ = z.shape
    f32 = jnp.float32
    xbc = jax.nn.silu(causal_depthwise_conv(xbc, conv_w, conv_b)).astype(f32)
    xs = xbc[..., :D_SSD]
    Bm = xbc[..., D_SSD:D_SSD + SSD_GROUPS * SSD_STATE].reshape(b, s, SSD_GROUPS, SSD_STATE)
    Cm = xbc[..., D_SSD + SSD_GROUPS * SSD_STATE:].reshape(b, s, SSD_GROUPS, SSD_STATE)
    xh = xs.reshape(b, s, SSD_GROUPS, SSD_HPG, HEAD_DIM)
    dt = jax.nn.softplus(dt_raw.astype(f32) + dt_bias.astype(f32)).reshape(b, s, SSD_GROUPS, SSD_HPG)
    A = -jnp.exp(A_log.astype(f32)).reshape(SSD_GROUPS, SSD_HPG)
    y = ssd_chunked(xh, dt, A, Bm, Cm)
    y = y + D_skip.astype(f32).reshape(SSD_GROUPS, SSD_HPG)[..., None] * xh
    y = y.reshape(b, s, D_SSD) * jax.nn.silu(z.astype(f32))
    yg = y.reshape(b, s, SSD_GROUPS, D_SSD // SSD_GROUPS)
    yg = yg * lax.rsqrt(jnp.mean(yg * yg, axis=-1, keepdims=True) + GATED_NORM_EPS)
    return (yg.reshape(b, s, D_SSD) * norm_g.astype(f32)).astype(z.dtype)


def wkv7_scan(r, w, k, v, a, bb):
    bsz, _, h, n = r.shape

    def step(S, inp):
        r_t, w_t, k_t, v_t, a_t, b_t = inp
        sa = jnp.einsum('bhvk,bhk->bhv', S, a_t)
        S = S * w_t[:, :, None, :] + sa[..., None] * b_t[:, :, None, :] + v_t[..., None] * k_t[:, :, None, :]
        return S, jnp.einsum('bhvk,bhk->bhv', S, r_t)

    seq = (r.swapaxes(0, 1), w.swapaxes(0, 1), k.swapaxes(0, 1), v.swapaxes(0, 1),
           a.swapaxes(0, 1), bb.swapaxes(0, 1))
    S0 = jnp.zeros((bsz, h, n, n), jnp.float32)
    _, y = lax.scan(step, S0, seq)
    return y.swapaxes(0, 1)


def rwkv7_mixer(r, k, v, wd, ad, gd, w0, w2, a0, a2, g2, k_k, k_a, r_k, gn_w, gn_b):
    f32 = jnp.float32
    b, s, _ = r.shape
    H, N = RWKV_HEADS, HEAD_DIM
    r, k, v, wd, ad, gd = (t.astype(f32) for t in (r, k, v, wd, ad, gd))
    logw = -jax.nn.softplus(-(w0.astype(f32) + jnp.tanh(wd) @ w2.astype(f32))) - 0.5
    decay = jnp.exp(-jnp.exp(logw))
    a = jax.nn.sigmoid(a0.astype(f32) + ad @ a2.astype(f32))
    g = jax.nn.sigmoid(gd) @ g2.astype(f32)
    kk = (k * k_k.astype(f32)).reshape(b, s, H, N)
    kk = kk / jnp.maximum(jnp.sqrt(jnp.sum(kk * kk, axis=-1, keepdims=True)), 1e-12)
    k = k * (1.0 + (a - 1.0) * k_a.astype(f32))
    rh = r.reshape(b, s, H, N)
    kh = k.reshape(b, s, H, N)
    vh = v.reshape(b, s, H, N)
    ah = a.reshape(b, s, H, N)
    y = wkv7_scan(rh, decay.reshape(b, s, H, N), kh, vh, -kk, kk * ah)
    mu = jnp.mean(y, axis=-1, keepdims=True)
    var = jnp.mean(jnp.square(y - mu), axis=-1, keepdims=True)
    y = ((y - mu) * lax.rsqrt(var + GN_EPS)).reshape(b, s, D_RWKV) * gn_w.astype(f32) + gn_b.astype(f32)
    bonus = jnp.sum(rh * kh * r_k.astype(f32).reshape(H, N), axis=-1, keepdims=True) * vh
    y = y + bonus.reshape(b, s, D_RWKV)
    return y * g


def setup_inputs(seed: int = 0) -> dict:
    key = jax.random.key(seed)
    ks = jax.random.split(key, 32)
    f32 = jnp.float32
    L = DEPTH

    def nrm(k, shape, scale):
        return jax.random.normal(k, shape, f32) * scale

    dt0 = jnp.exp(jax.random.uniform(ks[5], (L, SSD_HEADS), f32, np.log(1e-3), np.log(1e-1)))
    w0_base = jnp.linspace(-6.0, -1.0, D_RWKV, dtype=f32) + 0.5
    return {
        "x": jax.random.normal(ks[0], (BATCH, SEQ, D_MODEL), f32),
        "norm1_g": 1.0 + nrm(ks[1], (L, D_MODEL), 0.02),
        "w_in": nrm(ks[2], (L, D_MODEL, D_IN), D_MODEL ** -0.5),
        "ssd_conv_w": nrm(ks[3], (L, CONV_WIDTH, D_XBC), CONV_WIDTH ** -0.5),
        "ssd_conv_b": nrm(ks[4], (L, D_XBC), 0.02),
        "ssd_dt_bias": dt0 + jnp.log(-jnp.expm1(-dt0)),
        "ssd_A_log": jnp.log(jax.random.uniform(ks[6], (L, SSD_HEADS), f32, 1.0, 16.0)),
        "ssd_D": 1.0 + nrm(ks[7], (L, SSD_HEADS), 0.1),
        "ssd_norm_g": 1.0 + nrm(ks[8], (L, D_SSD), 0.02),
        "rwkv_mu": jax.random.uniform(ks[9], (L, D_RWKV_IN), f32),
        "rwkv_w0": w0_base + nrm(ks[10], (L, D_RWKV), 0.1),
        "rwkv_w2": nrm(ks[11], (L, DECAY_LORA, D_RWKV), 0.1 * DECAY_LORA ** -0.5),
        "rwkv_a0": nrm(ks[12], (L, D_RWKV), 0.1),
        "rwkv_a2": nrm(ks[13], (L, AAA_LORA, D_RWKV), 0.1 * AAA_LORA ** -0.5),
        "rwkv_g2": nrm(ks[14], (L, GATE_LORA, D_RWKV), GATE_LORA ** -0.5),
        "rwkv_k_k": 0.85 + nrm(ks[15], (L, D_RWKV), 0.05),
        "rwkv_k_a": 1.0 + nrm(ks[16], (L, D_RWKV), 0.05),
        "rwkv_r_k": nrm(ks[17], (L, D_RWKV), 0.1),
        "rwkv_gn_w": 1.0 + nrm(ks[18], (L, D_RWKV), 0.02),
        "rwkv_gn_b": nrm(ks[19], (L, D_RWKV), 0.02),
        "w_out": nrm(ks[20], (L, D_MIX, D_MODEL), D_MIX ** -0.5),
        "norm2_g": 1.0 + nrm(ks[21], (L, D_MODEL), 0.02),
        "w_gate": nrm(ks[22], (L, D_MODEL, D_FF), D_MODEL ** -0.5),
        "w_up": nrm(ks[23], (L, D_MODEL, D_FF), D_MODEL ** -0.5),
        "w_down": nrm(ks[24], (L, D_FF, D_MODEL), D_FF ** -0.5),
        "norm_f_g": 1.0 + nrm(ks[25], (D_MODEL,), 0.02),
    }


def reference(x, norm1_g, w_in, ssd_conv_w, ssd_conv_b, ssd_dt_bias, ssd_A_log, ssd_D, ssd_norm_g,
              rwkv_mu, rwkv_w0, rwkv_w2, rwkv_a0, rwkv_a2, rwkv_g2, rwkv_k_k, rwkv_k_a, rwkv_r_k,
              rwkv_gn_w, rwkv_gn_b, w_out, norm2_g, w_gate, w_up, w_down, norm_f_g):
    h = x
    for l in range(DEPTH):
        u = rms_norm(h, norm1_g[l])
        P = u @ w_in[l]
        o1 = D_SSD
        o2 = o1 + D_XBC
        o3 = o2 + SSD_HEADS
        z = P[..., :o1]
        xbc = P[..., o1:o2]
        dt_raw = P[..., o2:o3]
        pr = P[..., o3:]
        prev = jnp.pad(pr, ((0, 0), (1, 0), (0, 0)))[:, :-1]
        pr = pr + (prev - pr) * rwkv_mu[l]
        c1 = D_RWKV
        c2 = 2 * D_RWKV
        c3 = 3 * D_RWKV
        c4 = c3 + DECAY_LORA
        c5 = c4 + AAA_LORA
        y_ssd = ssd_mixer(z, xbc, dt_raw, ssd_conv_w[l], ssd_conv_b[l], ssd_dt_bias[l],
                          ssd_A_log[l], ssd_D[l], ssd_norm_g[l])
        y_rwkv = rwkv7_mixer(pr[..., :c1], pr[..., c1:c2], pr[..., c2:c3], pr[..., c3:c4],
                             pr[..., c4:c5], pr[..., c5:], rwkv_w0[l], rwkv_w2[l], rwkv_a0[l],
                             rwkv_a2[l], rwkv_g2[l], rwkv_k_k[l], rwkv_k_a[l], rwkv_r_k[l],
                             rwkv_gn_w[l], rwkv_gn_b[l]).astype(h.dtype)
        h = h + jnp.concatenate([y_ssd, y_rwkv], axis=-1) @ w_out[l]
        v = rms_norm(h, norm2_g[l])
        h = h + (jax.nn.silu(v @ w_gate[l]) * (v @ w_up[l])) @ w_down[l]
    return rms_norm(h, norm_f_g)
```

```python
import functools

import jax
import jax.numpy as jnp
from jax import lax
from jax.experimental import pallas as pl
from jax.experimental.pallas import tpu as pltpu

F32 = jnp.float32
BF16 = jnp.bfloat16
HIGHEST = lax.Precision.HIGHEST

HEAD_DIM = 64
CHUNK = 64
SSD_GROUPS = 4
SSD_STATE = 128
CONV_WIDTH = 4
RMS_EPS = 1e-6
GATED_NORM_EPS = 1e-5
GN_EPS = 64e-5

LANES = 128
SUBLANES = 8
VMEM_LIMIT_BYTES = 56 * 1024 * 1024

NT_DIMS = (((1,), (1,)), ((), ()))
TN_DIMS = (((0,), (0,)), ((), ()))


def _sigmoid(x):
    return jax.nn.sigmoid(x)


def _softplus(x):
    return jnp.maximum(x, 0.0) + jnp.log1p(jnp.exp(-jnp.abs(x)))


def _rms(h, eps):
    return h * lax.rsqrt(jnp.mean(h * h, axis=-1, keepdims=True) + eps)


def _params(*sem):
    return pltpu.CompilerParams(dimension_semantics=sem, vmem_limit_bytes=VMEM_LIMIT_BYTES)


def _rmsnorm_kernel(x_ref, g_ref, o_ref):
    o_ref[...] = (_rms(x_ref[...], RMS_EPS) * g_ref[...]).astype(o_ref.dtype)


def _rmsnorm(x, g, tm):
    m, d = x.shape
    return pl.pallas_call(
        _rmsnorm_kernel,
        grid=(m // tm,),
        in_specs=[pl.BlockSpec((tm, d), lambda i: (i, 0)), pl.BlockSpec((1, d), lambda i: (0, 0))],
        out_specs=pl.BlockSpec((tm, d), lambda i: (i, 0)),
        out_shape=jax.ShapeDtypeStruct((m, d), BF16),
        compiler_params=_params("parallel"),
        name="rmsnorm",
    )(x, g.reshape(1, d))


def _matmul_kernel(*refs, nk, has_res, norm_mode):
    it = iter(refs)
    a_ref, w_ref = next(it), next(it)
    res_ref = next(it) if has_res else None
    g_ref = next(it) if norm_mode else None
    o_ref = next(it)
    o2_ref = next(it) if norm_mode == "aux" else None
    acc_ref = next(it) if nk > 1 else None

    def epilogue(h):
        if has_res:
            h = h + res_ref[...]
        if norm_mode == "final":
            o_ref[...] = (_rms(h, RMS_EPS) * g_ref[...]).astype(o_ref.dtype)
        else:
            o_ref[...] = h.astype(o_ref.dtype)
            if norm_mode == "aux":
                o2_ref[...] = (_rms(h, RMS_EPS) * g_ref[...]).astype(o2_ref.dtype)

    part = jnp.dot(a_ref[...], w_ref[...], preferred_element_type=F32)
    if nk == 1:
        epilogue(part)
        return
    k = pl.program_id(2)

    @pl.when(k == 0)
    def _():
        acc_ref[...] = part

    @pl.when(k > 0)
    def _():
        acc_ref[...] += part

    @pl.when(k == nk - 1)
    def _():
        epilogue(acc_ref[...])


def _matmul(a, w, *, tm, tn, tk, out_dtype=F32, res=None, norm_g=None, norm_mode=None, name="matmul"):
    m, kd = a.shape
    n = w.shape[1]
    nk = kd // tk
    if norm_mode:
        assert tn == n, "row rmsnorm needs the whole row in one block"
    in_specs = [pl.BlockSpec((tm, tk), lambda i, j, k: (i, k)), pl.BlockSpec((tk, tn), lambda i, j, k: (k, j))]
    args = [a, w]
    if res is not None:
        in_specs.append(pl.BlockSpec((tm, tn), lambda i, j, k: (i, j)))
        args.append(res)
    if norm_mode:
        in_specs.append(pl.BlockSpec((1, tn), lambda i, j, k: (0, j)))
        args.append(norm_g.reshape(1, n))
    out_spec = pl.BlockSpec((tm, tn), lambda i, j, k: (i, j))
    out_shape = jax.ShapeDtypeStruct((m, n), out_dtype)
    if norm_mode == "aux":
        out_specs, out_shapes = [out_spec, out_spec], [out_shape, jax.ShapeDtypeStruct((m, n), BF16)]
    else:
        out_specs, out_shapes = out_spec, out_shape
    return pl.pallas_call(
        functools.partial(_matmul_kernel, nk=nk, has_res=res is not None, norm_mode=norm_mode),
        grid=(m // tm, n // tn, nk),
        in_specs=in_specs,
        out_specs=out_specs,
        out_shape=out_shapes,
        scratch_shapes=[pltpu.VMEM((tm, tn), F32)] if nk > 1 else [],
        compiler_params=_params("parallel", "parallel", "arbitrary"),
        name=name,
    )(*args)


def _gate_up_kernel(a_ref, wg_ref, wu_ref, o_ref):
    a = a_ref[...]
    gate = jnp.dot(a, wg_ref[...], preferred_element_type=F32)
    up = jnp.dot(a, wu_ref[...], preferred_element_type=F32)
    o_ref[...] = (gate * _sigmoid(gate) * up).astype(o_ref.dtype)


def _gate_up(a, wg, wu, *, tm, tn):
    m, kd = a.shape
    n = wg.shape[1]
    return pl.pallas_call(
        _gate_up_kernel,
        grid=(m // tm, n // tn),
        in_specs=[pl.BlockSpec((tm, kd), lambda i, j: (i, 0)),
                  pl.BlockSpec((kd, tn), lambda i, j: (0, j)),
                  pl.BlockSpec((kd, tn), lambda i, j: (0, j))],
        out_specs=pl.BlockSpec((tm, tn), lambda i, j: (i, j)),
        out_shape=jax.ShapeDtypeStruct((m, n), BF16),
        compiler_params=_params("parallel", "parallel"),
        name="gate_up",
    )(a, wg, wu)


def _shift_rows(cur, prev8, d, first):
    rolled = pltpu.roll(cur, d, axis=0)
    prolled = jnp.where(first, 0.0, pltpu.roll(prev8, d, axis=0))
    row = lax.broadcasted_iota(jnp.int32, prolled.shape, 0)
    top = jnp.where(row < d, prolled, rolled[0:SUBLANES])
    return jnp.concatenate([top, rolled[SUBLANES:]], axis=0)


def _ssd_kernel(z_ref, xs_ref, bc_ref, xsp_ref, bcp_ref, dt_ref, cwx_ref, cwbc_ref, cbx_ref, cbbc_ref,
                dtb_ref, alog_ref, dexp_ref, ng_ref, e_ref, tril_ref, o_ref, h_ref, y_ref, *, heads):
    first = pl.program_id(1) == 0
    hpg = heads // SSD_GROUPS
    gw = hpg * HEAD_DIM

    @pl.when(first)
    def _():
        h_ref[...] = jnp.zeros_like(h_ref)

    def conv_silu(cur, prev8, w_ref, b_ref):
        acc = cur * w_ref[CONV_WIDTH - 1:CONV_WIDTH, :] + b_ref[...]
        for d in range(1, CONV_WIDTH):
            acc = acc + _shift_rows(cur, prev8, d, first) * w_ref[CONV_WIDTH - 1 - d:CONV_WIDTH - d, :]
        return acc * _sigmoid(acc)

    xs = conv_silu(xs_ref[...], xsp_ref[...], cwx_ref, cbx_ref)
    bc = conv_silu(bc_ref[...], bcp_ref[...], cwbc_ref, cbbc_ref)
    xs_bf = xs.astype(BF16)

    dt = _softplus(dt_ref[...] + dtb_ref[...])
    adt = dt * (-jnp.exp(alog_ref[...]))
    cs = jnp.dot(tril_ref[...], adt, precision=HIGHEST, preferred_element_type=F32)
    cs_last = cs[CHUNK - 1:CHUNK, :]
    decay_states = jnp.exp(cs_last - cs)
    chunk_decay = jnp.exp(cs_last)
    exp_in = jnp.concatenate([dt * decay_states, jnp.exp(cs), jnp.broadcast_to(chunk_decay, (SUBLANES, LANES))],
                             axis=0)
    expanded = jnp.dot(exp_in, e_ref[...], precision=HIGHEST, preferred_element_type=F32)
    xscale = expanded[0:CHUNK]
    ecs = expanded[CHUNK:2 * CHUNK]
    cdecay = expanded[2 * CHUNK:2 * CHUNK + 1]

    cs_t = cs.T
    dt_t = dt.T
    li = lax.broadcasted_iota(jnp.int32, (CHUNK, CHUNK), 0)
    si = lax.broadcasted_iota(jnp.int32, (CHUNK, CHUNK), 1)
    causal = li >= si
    lane = lax.broadcasted_iota(jnp.int32, (CHUNK, LANES), 1)

    cbs = []
    for g in range(SSD_GROUPS):
        bm = bc[:, g * SSD_STATE:(g + 1) * SSD_STATE].astype(BF16)
        cm = bc[:, (SSD_GROUPS + g) * SSD_STATE:(SSD_GROUPS + g + 1) * SSD_STATE].astype(BF16)
        cbs.append(lax.dot_general(cm, bm, NT_DIMS, preferred_element_type=F32))
        cols = slice(g * gw, (g + 1) * gw)
        xsc = (xs[:, cols] * xscale[:, cols]).astype(BF16)
        states = lax.dot_general(bm, xsc, TN_DIMS, preferred_element_type=F32)
        h = h_ref[g]
        y_ref[:, cols] = jnp.dot(cm, h.astype(BF16), preferred_element_type=F32) * ecs[:, cols]
        h_ref[g] = h * cdecay[:, cols] + states

    for q in range(heads // 2):
        mats = []
        for e in (2 * q, 2 * q + 1):
            diff = cs[:, e:e + 1] - cs_t[e:e + 1, :]
            lmat = jnp.exp(jnp.where(causal, diff, -jnp.inf))
            mats.append(cbs[e // hpg] * lmat * dt_t[e:e + 1, :])
        lhs = jnp.concatenate(mats, axis=0).astype(BF16)
        cols = slice(q * LANES, (q + 1) * LANES)
        res = jnp.dot(lhs, xs_bf[:, cols], preferred_element_type=F32)
        y_ref[:, cols] += jnp.where(lane < HEAD_DIM, res[0:CHUNK], res[CHUNK:2 * CHUNK])

    z = z_ref[...]
    y = (y_ref[...] + dexp_ref[...] * xs) * (z * _sigmoid(z))
    for g in range(SSD_GROUPS):
        cols = slice(g * gw, (g + 1) * gw)
        o_ref[:, cols] = (_rms(y[:, cols], GATED_NORM_EPS) * ng_ref[:, cols]).astype(o_ref.dtype)


def _ssd_mixer(p_main, p_small, conv_w, conv_b, dt_bias, a_log, d_skip, norm_g, *, d_ssd, col_xs, col_bc, col_dt):
    b, s, _ = p_main.shape
    heads = d_ssd // HEAD_DIM
    d_bc = 2 * SSD_GROUPS * SSD_STATE
    assert s % CHUNK == 0 and heads % (2 * SSD_GROUPS) == 0 and heads <= LANES
    assert col_xs % d_ssd == 0 and col_bc % d_bc == 0 and col_dt % LANES == 0
    rows8 = CHUNK // SUBLANES

    def prev_map(cb):
        return lambda bi, i: (bi, jnp.maximum(i * rows8 - 1, 0), cb)

    def const(shape):
        return pl.BlockSpec(shape, lambda bi, i: (0,) * len(shape))

    pad = LANES - heads
    e_mat = (jnp.arange(LANES)[:, None] == (jnp.arange(d_ssd)[None, :] // HEAD_DIM)).astype(F32)
    tril = jnp.tril(jnp.ones((CHUNK, CHUNK), F32))
    return pl.pallas_call(
        functools.partial(_ssd_kernel, heads=heads),
        grid=(b, s // CHUNK),
        in_specs=[
            pl.BlockSpec((None, CHUNK, d_ssd), lambda bi, i: (bi, i, 0)),
            pl.BlockSpec((None, CHUNK, d_ssd), lambda bi, i: (bi, i, col_xs // d_ssd)),
            pl.BlockSpec((None, CHUNK, d_bc), lambda bi, i: (bi, i, col_bc // d_bc)),
            pl.BlockSpec((None, SUBLANES, d_ssd), prev_map(col_xs // d_ssd)),
            pl.BlockSpec((None, SUBLANES, d_bc), prev_map(col_bc // d_bc)),
            pl.BlockSpec((None, CHUNK, LANES), lambda bi, i: (bi, i, col_dt // LANES)),
            const((CONV_WIDTH, d_ssd)), const((CONV_WIDTH, d_bc)), const((1, d_ssd)), const((1, d_bc)),
            const((1, LANES)), const((1, LANES)), const((1, d_ssd)), const((1, d_ssd)),
            const((LANES, d_ssd)), const((CHUNK, CHUNK)),
        ],
        out_specs=pl.BlockSpec((None, CHUNK, d_ssd), lambda bi, i: (bi, i, 0)),
        out_shape=jax.ShapeDtypeStruct((b, s, d_ssd), BF16),
        scratch_shapes=[pltpu.VMEM((SSD_GROUPS, SSD_STATE, d_ssd // SSD_GROUPS), F32),
                        pltpu.VMEM((CHUNK, d_ssd), F32)],
        compiler_params=_params("parallel", "arbitrary"),
        name="ssd_mixer",
    )(p_main, p_main, p_main, p_main, p_main, p_small,
      conv_w[:, :d_ssd], conv_w[:, d_ssd:], conv_b[None, :d_ssd], conv_b[None, d_ssd:],
      jnp.pad(dt_bias, (0, pad))[None], jnp.pad(a_log, (0, pad))[None],
      jnp.repeat(d_skip, HEAD_DIM)[None], norm_g[None], e_mat, tril)


def _rwkv_prep_kernel(r_ref, k_ref, v_ref, sm_ref, rp_ref, kp_ref, vp_ref, smp_ref,
                      mur_ref, muk_ref, muv_ref, musm_ref, w0_ref, a0_ref, w2_ref, a2_ref, g2_ref,
                      ro_ref, ko_ref, vo_ref, lw_ref, ao_ref, go_ref, *, lora_w, lora_a):
    first = pl.program_id(1) == 0

    def token_shift(cur_ref, prev_ref, mu_ref):
        cur = cur_ref[...]
        prev = _shift_rows(cur, prev_ref[...], 1, first)
        return cur + (prev - cur) * mu_ref[...]

    ro_ref[...] = token_shift(r_ref, rp_ref, mur_ref)
    ko_ref[...] = token_shift(k_ref, kp_ref, muk_ref)
    vo_ref[...] = token_shift(v_ref, vp_ref, muv_ref)
    sm = token_shift(sm_ref, smp_ref, musm_ref)
    wd = sm[:, 0:lora_w]
    ad = sm[:, lora_w:lora_w + lora_a]
    gd = sm[:, lora_w + lora_a:]
    dw = jnp.dot(jnp.tanh(wd), w2_ref[...], precision=HIGHEST, preferred_element_type=F32)
    logw = -_softplus(-(w0_ref[...] + dw)) - 0.5
    lw_ref[...] = -jnp.exp(logw)
    da = jnp.dot(ad, a2_ref[...], precision=HIGHEST, preferred_element_type=F32)
    ao_ref[...] = _sigmoid(a0_ref[...] + da)
    go_ref[...] = jnp.dot(_sigmoid(gd).astype(BF16), g2_ref[...], preferred_element_type=F32)


def _rwkv_prep(p_main, p_small, mu_r, mu_k, mu_v, mu_sm, w0, a0, w2p, a2p, g2, *, d, col_r, tl):
    b, s, _ = p_main.shape
    lw_, la_, lg_ = w2p.shape[0], a2p.shape[0], g2.shape[0]
    wsm = lw_ + la_ + lg_
    assert col_r % d == 0 and s % tl == 0
    rows8 = tl // SUBLANES
    cr = col_r // d

    def cur(width, cb):
        return pl.BlockSpec((None, tl, width), lambda bi, i: (bi, i, cb))

    def prev(width, cb):
        return pl.BlockSpec((None, SUBLANES, width), lambda bi, i: (bi, jnp.maximum(i * rows8 - 1, 0), cb))

    def const(shape):
        return pl.BlockSpec(shape, lambda bi, i: (0,) * len(shape))

    out = jax.ShapeDtypeStruct((b, s, d), F32)
    return pl.pallas_call(
        functools.partial(_rwkv_prep_kernel, lora_w=lw_, lora_a=la_),
        grid=(b, s // tl),
        in_specs=[cur(d, cr), cur(d, cr + 1), cur(d, cr + 2), cur(wsm, 0),
                  prev(d, cr), prev(d, cr + 1), prev(d, cr + 2), prev(wsm, 0),
                  const((1, d)), const((1, d)), const((1, d)), const((1, wsm)), const((1, d)), const((1, d)),
                  const((lw_, d)), const((la_, d)), const((lg_, d))],
        out_specs=[cur(d, 0)] * 6,
        out_shape=[out] * 6,
        compiler_params=_params("parallel", "parallel"),
        name="rwkv_prep",
    )(p_main, p_main, p_main, p_small, p_main, p_main, p_main, p_small,
      mu_r[None], mu_k[None], mu_v[None], mu_sm[None], w0[None], a0[None], w2p, a2p, g2)


def _stack_heads(x, lane_lo):
    return jnp.concatenate([jnp.where(lane_lo, x, 0.0), jnp.where(lane_lo, 0.0, x)], axis=0)


def _rwkv_scan_kernel(r_ref, k_ref, v_ref, lw_ref, a_ref, g_ref, kk_ref, ka_ref, rk_ref, gw_ref, gb_ref,
                      tril_ref, bd_ref, o_ref, st_ref, *, pairs, chunks):
    @pl.when(pl.program_id(2) == 0)
    def _():
        st_ref[...] = jnp.zeros_like(st_ref)

    two_l = 2 * CHUNK
    ri = lax.broadcasted_iota(jnp.int32, (two_l, two_l), 0)
    ci = lax.broadcasted_iota(jnp.int32, (two_l, two_l), 1)
    same_head = (ri // CHUNK) == (ci // CHUNK)
    strict = same_head & (ci < ri)
    incl = same_head & (ci <= ri)
    eye = (ri == ci).astype(F32)
    lane_lo = lax.broadcasted_iota(jnp.int32, (CHUNK, LANES), 1) < HEAD_DIM

    def head_sum(x):
        return jnp.dot(x, bd_ref[...], precision=HIGHEST, preferred_element_type=F32)

    def chunk_body(c, carry):
        rows = pl.ds(pl.multiple_of(c * CHUNK, CHUNK), CHUNK)
        for p in range(pairs):
            cols = slice(p * LANES, (p + 1) * LANES)
            r = r_ref[rows, cols]
            k = k_ref[rows, cols]
            v = v_ref[rows, cols]
            lw = lw_ref[rows, cols]
            a = a_ref[rows, cols]
            kk = k * kk_ref[:, cols]
            kk = kk / jnp.maximum(jnp.sqrt(head_sum(kk * kk)), 1e-12)
            k2 = k * (1.0 + (a - 1.0) * ka_ref[:, cols])
            cl = jnp.dot(tril_ref[...], lw, precision=HIGHEST, preferred_element_type=F32)
            e_pos = jnp.exp(cl)
            e_neg = jnp.exp(-cl)
            a_s = _stack_heads(-kk * jnp.exp(cl - lw), lane_lo).astype(BF16)
            r_s = _stack_heads(r * e_pos, lane_lo).astype(BF16)
            b_s = _stack_heads(kk * a * e_neg, lane_lo).astype(BF16)
            k_s = _stack_heads(k2 * e_neg, lane_lo).astype(BF16)
            v_s = _stack_heads(v, lane_lo).astype(BF16)
            g_last = e_pos[CHUNK - 1:CHUNK, :]

            ar = jnp.concatenate([a_s, r_s], axis=0)
            bk = jnp.concatenate([b_s, k_s], axis=0)
            att = lax.dot_general(ar, bk, NT_DIMS, preferred_element_type=F32)
            a_ab = jnp.where(strict, att[0:two_l, 0:two_l], 0.0)
            a_ak = jnp.where(strict, att[0:two_l, two_l:], 0.0)
            a_rb = jnp.where(incl, att[two_l:, 0:two_l], 0.0)
            a_rk = jnp.where(incl, att[two_l:, two_l:], 0.0)

            tinv = eye + jnp.where((ri // 2) == (ci // 2), a_ab, 0.0)
            m = 2
            while m < CHUNK:
                off = jnp.where(((ri // (2 * m)) == (ci // (2 * m))) & ((ri // m) != (ci // m)), a_ab, 0.0)
                tb = tinv.astype(BF16)
                mid = jnp.dot(tb, off.astype(BF16), preferred_element_type=F32)
                tinv = tinv + jnp.dot(mid.astype(BF16), tb, preferred_element_type=F32)
                m *= 2

            st = st_ref[p]
            st_bf = st.astype(BF16)
            x1 = lax.dot_general(a_s, st_bf, NT_DIMS, preferred_element_type=F32)
            x2 = jnp.dot(a_ak.astype(BF16), v_s, preferred_element_type=F32)
            u = jnp.dot(tinv.astype(BF16), (x1 + x2).astype(BF16), preferred_element_type=F32)
            u_bf = u.astype(BF16)
            ys = (lax.dot_general(r_s, st_bf, NT_DIMS, preferred_element_type=F32)
                  + jnp.dot(a_rb.astype(BF16), u_bf, preferred_element_type=F32)
                  + jnp.dot(a_rk.astype(BF16), v_s, preferred_element_type=F32))
            upd = (lax.dot_general(u_bf, b_s, TN_DIMS, preferred_element_type=F32)
                   + lax.dot_general(v_s, k_s, TN_DIMS, preferred_element_type=F32))
            st_ref[p] = (st + upd) * g_last

            y = ys[0:CHUNK] + ys[CHUNK:]
            mean = head_sum(y) * (1.0 / HEAD_DIM)
            yc = y - mean
            var = head_sum(yc * yc) * (1.0 / HEAD_DIM)
            yn = yc * lax.rsqrt(var + GN_EPS) * gw_ref[:, cols] + gb_ref[:, cols]
            bonus = head_sum(r * k2 * rk_ref[:, cols]) * v
            o_ref[rows, cols] = ((yn + bonus) * g_ref[rows, cols]).astype(o_ref.dtype)
        return carry

    lax.fori_loop(0, chunks, chunk_body, 0)


def _rwkv_scan(r, k, v, lw, a, g, k_k, k_a, r_k, gn_w, gn_b, *, tl, pairs):
    b, s, d = r.shape
    width = pairs * LANES
    assert HEAD_DIM * 2 == LANES and d % width == 0 and s % tl == 0 and tl % CHUNK == 0

    seq = pl.BlockSpec((None, tl, width), lambda bi, p, i: (bi, i, p))
    par = pl.BlockSpec((1, width), lambda bi, p, i: (0, p))

    def const(shape):
        return pl.BlockSpec(shape, lambda bi, p, i: (0,) * len(shape))

    tril = jnp.tril(jnp.ones((CHUNK, CHUNK), F32))
    blk = jnp.arange(LANES) // HEAD_DIM
    bd = (blk[:, None] == blk[None, :]).astype(F32)
    return pl.pallas_call(
        functools.partial(_rwkv_scan_kernel, pairs=pairs, chunks=tl // CHUNK),
        grid=(b, d // width, s // tl),
        in_specs=[seq] * 6 + [par] * 5 + [const((CHUNK, CHUNK)), const((LANES, LANES))],
        out_specs=seq,
        out_shape=jax.ShapeDtypeStruct((b, s, d), BF16),
        scratch_shapes=[pltpu.VMEM((pairs, LANES, LANES), F32)],
        compiler_params=_params("parallel", "parallel", "arbitrary"),
        name="rwkv_scan",
    )(r, k, v, lw, a, g, k_k[None], k_a[None], r_k[None], gn_w[None], gn_b[None], tril, bd)


def _pad_cols(w, n):
    return jnp.pad(w, ((0, 0), (0, n - w.shape[1])))


def _pad_rows(w, n):
    return jnp.pad(w, ((0, n - w.shape[0]), (0, 0)))


def _pad_vec(v, n):
    return jnp.pad(v, (0, n - v.shape[0]))


def _round_up(n, m):
    return -(-n // m) * m


def _layer(h, norm1_g, w_in, conv_w, conv_b, dt_bias, a_log, d_skip, ssd_norm_g, mu, w0, w2, a0, a2, g2,
           k_k, k_a, r_k, gn_w, gn_b, w_out, norm2_g, w_gate, w_up, w_down, out_g, final):
    b, s, d = h.shape
    m = b * s
    d_ssd = ssd_norm_g.shape[0]
    d_rwkv = w0.shape[0]
    d_bc = 2 * SSD_GROUPS * SSD_STATE
    heads = d_ssd // HEAD_DIM
    lw_, la_, lg_ = w2.shape[0], a2.shape[0], g2.shape[0]
    lwp, lap, lgp = (_round_up(n, LANES) for n in (lw_, la_, lg_))

    o_xbc = d_ssd
    o_dt = o_xbc + d_ssd + d_bc
    o_r = o_dt + heads
    o_wd = o_r + 3 * d_rwkv
    o_ad = o_wd + lw_
    o_gd = o_ad + la_
    w_main = jnp.concatenate([w_in[:, :d_ssd], w_in[:, o_xbc:o_xbc + d_ssd], w_in[:, o_r:o_wd],
                              w_in[:, o_xbc + d_ssd:o_dt]], axis=1).astype(BF16)
    w_small = jnp.concatenate([_pad_cols(w_in[:, o_wd:o_ad], lwp), _pad_cols(w_in[:, o_ad:o_gd], lap),
                               _pad_cols(w_in[:, o_gd:], lgp), _pad_cols(w_in[:, o_dt:o_r], LANES)],
                              axis=1).astype(BF16)
    col_xs, col_r, col_bc = d_ssd, 2 * d_ssd, 2 * d_ssd + 3 * d_rwkv
    col_dt = lwp + lap + lgp
    mu_sm = jnp.concatenate([_pad_vec(mu[3 * d_rwkv:3 * d_rwkv + lw_], lwp),
                             _pad_vec(mu[3 * d_rwkv + lw_:3 * d_rwkv + lw_ + la_], lap),
                             _pad_vec(mu[3 * d_rwkv + lw_ + la_:], lgp)])

    tm = min(1024, m)
    u = _rmsnorm(h.reshape(m, d), norm1_g, tm=min(512, m))
    p_main = _matmul(u, w_main, tm=tm, tn=1024, tk=d, name="in_proj_main").reshape(b, s, -1)
    p_small = _matmul(u, w_small, tm=tm, tn=w_small.shape[1], tk=d, name="in_proj_small").reshape(b, s, -1)

    y_ssd = _ssd_mixer(p_main, p_small, conv_w, conv_b, dt_bias, a_log, d_skip, ssd_norm_g,
                       d_ssd=d_ssd, col_xs=col_xs, col_bc=col_bc, col_dt=col_dt)

    p_lora = p_small[:, :, :col_dt] if col_dt != p_small.shape[-1] else p_small
    r, k, v, lw, a, g = _rwkv_prep(
        p_main, p_lora, mu[:d_rwkv], mu[d_rwkv:2 * d_rwkv], mu[2 * d_rwkv:3 * d_rwkv], mu_sm, w0, a0,
        _pad_rows(w2, lwp), _pad_rows(a2, lap), _pad_rows(g2, lgp).astype(BF16),
        d=d_rwkv, col_r=col_r, tl=min(256, s))
    y_rwkv = _rwkv_scan(r, k, v, lw, a, g, k_k, k_a, r_k, gn_w, gn_b, tl=min(512, s), pairs=2)

    mix = jnp.concatenate([y_ssd, y_rwkv], axis=-1).reshape(m, d_ssd + d_rwkv)
    tm2 = min(512, m)
    h1, v2 = _matmul(mix, w_out.astype(BF16), tm=tm2, tn=d, tk=1024, res=h.reshape(m, d), norm_g=norm2_g,
                     norm_mode="aux", name="out_proj")
    ff = _gate_up(v2, w_gate.astype(BF16), w_up.astype(BF16), tm=tm, tn=512)
    d_ff = w_down.shape[0]
    tk = 512 if d_ff % 512 == 0 else d_ff
    if final:
        out = _matmul(ff, w_down.astype(BF16), tm=tm2, tn=d, tk=tk, res=h1, norm_g=out_g, norm_mode="final",
                      name="down_proj")
    else:
        out = _matmul(ff, w_down.astype(BF16), tm=tm2, tn=d, tk=tk, res=h1, name="down_proj")
    return out.reshape(b, s, d)


def kernel(x, norm1_g, w_in, ssd_conv_w, ssd_conv_b, ssd_dt_bias, ssd_A_log, ssd_D, ssd_norm_g, rwkv_mu, rwkv_w0,
           rwkv_w2, rwkv_a0, rwkv_a2, rwkv_g2, rwkv_k_k, rwkv_k_a, rwkv_r_k, rwkv_gn_w, rwkv_gn_b, w_out,
           norm2_g, w_gate, w_up, w_down, norm_f_g):
    depth = w_in.shape[0]
    h = x
    for l in range(depth):
        h = _layer(h, norm1_g[l], w_in[l], ssd_conv_w[l], ssd_conv_b[l], ssd_dt_bias[l], ssd_A_log[l], ssd_D[l],
                   ssd_norm_g[l], rwkv_mu[l], rwkv_w0[l], rwkv_w2[l], rwkv_a0[l], rwkv_a2[l], rwkv_g2[l],
                   rwkv_k_k[l], rwkv_k_a[l], rwkv_r_k[l], rwkv_gn_w[l], rwkv_gn_b[l], w_out[l], norm2_g[l],
                   w_gate[l], w_up[l], w_down[l], norm_f_g, final=(l == depth - 1))
    return h
```

```python
import functools

import jax
import jax.numpy as jnp
from jax import lax
from jax.experimental import pallas as pl
from jax.experimental.pallas import tpu as pltpu

F32 = jnp.float32
BF16 = jnp.bfloat16
HIGHEST = lax.Precision.HIGHEST

HEAD_DIM = 64
CHUNK = 64
SSD_GROUPS = 4
SSD_STATE = 128
CONV_WIDTH = 4
RMS_EPS = 1e-6
GATED_NORM_EPS = 1e-5
GN_EPS = 64e-5

LANES = 128
SUBLANES = 8
VMEM_LIMIT_BYTES = 56 * 1024 * 1024

NT_DIMS = (((1,), (1,)), ((), ()))
TN_DIMS = (((0,), (0,)), ((), ()))


def _sigmoid(x):
    return jax.nn.sigmoid(x)


def _softplus(x):
    return jnp.maximum(x, 0.0) + jnp.log1p(jnp.exp(-jnp.abs(x)))


def _rms(h, eps):
    return h * lax.rsqrt(jnp.mean(h * h, axis=-1, keepdims=True) + eps)


def _params(*sem):
    return pltpu.CompilerParams(dimension_semantics=sem, vmem_limit_bytes=VMEM_LIMIT_BYTES)


def _mm(a, b):
    return jnp.dot(a, b, preferred_element_type=F32)


def _rmsnorm_kernel(x_ref, g_ref, o_ref):
    o_ref[...] = (_rms(x_ref[...], RMS_EPS) * g_ref[...]).astype(o_ref.dtype)


def _rmsnorm(x, g, tm):
    m, d = x.shape
    return pl.pallas_call(
        _rmsnorm_kernel,
        grid=(m // tm,),
        in_specs=[pl.BlockSpec((tm, d), lambda i: (i, 0)), pl.BlockSpec((1, d), lambda i: (0, 0))],
        out_specs=pl.BlockSpec((tm, d), lambda i: (i, 0)),
        out_shape=jax.ShapeDtypeStruct((m, d), BF16),
        compiler_params=_params("parallel"),
        name="rmsnorm",
    )(x, g.reshape(1, d))


def _matmul_kernel(a_ref, w_ref, o_ref):
    o_ref[...] = _mm(a_ref[...], w_ref[...]).astype(o_ref.dtype)


def _matmul(a, w, *, tm, tn, name):
    m, kd = a.shape
    n = w.shape[1]
    return pl.pallas_call(
        _matmul_kernel,
        grid=(m // tm, n // tn),
        in_specs=[pl.BlockSpec((tm, kd), lambda i, j: (i, 0)), pl.BlockSpec((kd, tn), lambda i, j: (0, j))],
        out_specs=pl.BlockSpec((tm, tn), lambda i, j: (i, j)),
        out_shape=jax.ShapeDtypeStruct((m, n), F32),
        compiler_params=_params("parallel", "parallel"),
        name=name,
    )(a, w)


def _row_matmul_kernel(*refs, n_in, nj, tn, final):
    a_refs, w_refs = refs[:n_in], refs[n_in:2 * n_in]
    res_ref, g_ref, h_ref = refs[2 * n_in:2 * n_in + 3]
    j = pl.program_id(1)
    part = _mm(a_refs[0][...], w_refs[0][...])
    for a_ref, w_ref in zip(a_refs[1:], w_refs[1:]):
        part = part + _mm(a_ref[...], w_ref[...])
    for jj in range(nj):
        @pl.when(j == jj)
        def _(jj=jj):
            cols = slice(jj * tn, (jj + 1) * tn)
            h_ref[:, cols] = part + res_ref[:, cols]

    @pl.when(j == nj - 1)
    def _():
        normed = _rms(h_ref[...], RMS_EPS) * g_ref[...]
        if final:
            h_ref[...] = normed
        else:
            refs[2 * n_in + 3][...] = normed.astype(BF16)


def _row_matmul(a_list, w, res, norm_g, *, tm, tn, final, name):
    m, n = res.shape
    kds = [a.shape[1] for a in a_list]
    assert sum(kds) == w.shape[0] and all(kd == kds[0] for kd in kds) and n % tn == 0
    nj = n // tn
    row = pl.BlockSpec((tm, n), lambda i, j: (i, 0))
    in_specs = [pl.BlockSpec((tm, kd), lambda i, j: (i, 0)) for kd in kds]
    in_specs += [pl.BlockSpec((kds[0], tn), lambda i, j, idx=idx: (idx, j)) for idx in range(len(kds))]
    in_specs += [row, pl.BlockSpec((1, n), lambda i, j: (0, 0))]
    h_shape = jax.ShapeDtypeStruct((m, n), F32)
    return pl.pallas_call(
        functools.partial(_row_matmul_kernel, n_in=len(kds), nj=nj, tn=tn, final=final),
        grid=(m // tm, nj),
        in_specs=in_specs,
        out_specs=row if final else [row, row],
        out_shape=h_shape if final else [h_shape, jax.ShapeDtypeStruct((m, n), BF16)],
        compiler_params=_params("parallel", "arbitrary"),
        name=name,
    )(*a_list, *([w] * len(kds)), res, norm_g.reshape(1, n))


def _gate_up_kernel(a_ref, wg_ref, wu_ref, o_ref):
    a = a_ref[...]
    gate = _mm(a, wg_ref[...])
    up = _mm(a, wu_ref[...])
    o_ref[...] = (gate * _sigmoid(gate) * up).astype(o_ref.dtype)


def _gate_up(a, wg, wu, *, tm, tn):
    m, kd = a.shape
    n = wg.shape[1]
    return pl.pallas_call(
        _gate_up_kernel,
        grid=(m // tm, n // tn),
        in_specs=[pl.BlockSpec((tm, kd), lambda i, j: (i, 0)),
                  pl.BlockSpec((kd, tn), lambda i, j: (0, j)),
                  pl.BlockSpec((kd, tn), lambda i, j: (0, j))],
        out_specs=pl.BlockSpec((tm, tn), lambda i, j: (i, j)),
        out_shape=jax.ShapeDtypeStruct((m, n), BF16),
        compiler_params=_params("parallel", "parallel"),
        name="gate_up",
    )(a, wg, wu)


def _shift_rows(cur, prev8, d, first):
    rolled = pltpu.roll(cur, d, axis=0)
    prolled = jnp.where(first, 0.0, pltpu.roll(prev8, d, axis=0))
    row = lax.broadcasted_iota(jnp.int32, prolled.shape, 0)
    top = jnp.where(row < d, prolled, rolled[0:SUBLANES])
    return jnp.concatenate([top, rolled[SUBLANES:]], axis=0)


def _ssd_kernel(z_ref, xs_ref, bc_ref, xsp_ref, bcp_ref, dt_ref, cwx_ref, cwbc_ref, cbx_ref, cbbc_ref,
                dtb_ref, alog_ref, dexp_ref, ng_ref, e_ref, tril_ref, o_ref, h_ref, y_ref, *, heads):
    first = pl.program_id(1) == 0
    hpg = heads // SSD_GROUPS
    gw = hpg * HEAD_DIM

    @pl.when(first)
    def _():
        h_ref[...] = jnp.zeros_like(h_ref)

    def conv_silu(cur, prev8, w_ref, b_ref):
        acc = cur * w_ref[CONV_WIDTH - 1:CONV_WIDTH, :] + b_ref[...]
        for d in range(1, CONV_WIDTH):
            acc = acc + _shift_rows(cur, prev8, d, first) * w_ref[CONV_WIDTH - 1 - d:CONV_WIDTH - d, :]
        return acc * _sigmoid(acc)

    xs = conv_silu(xs_ref[...], xsp_ref[...], cwx_ref, cbx_ref)
    bc = conv_silu(bc_ref[...], bcp_ref[...], cwbc_ref, cbbc_ref)
    xs_bf = xs.astype(BF16)

    dt = _softplus(dt_ref[...] + dtb_ref[...])
    adt = dt * (-jnp.exp(alog_ref[...]))
    cs = jnp.dot(tril_ref[...], adt, precision=HIGHEST, preferred_element_type=F32)
    cs_last = cs[CHUNK - 1:CHUNK, :]
    decay_states = jnp.exp(cs_last - cs)
    chunk_decay = jnp.exp(cs_last)
    exp_in = jnp.concatenate([dt * decay_states, jnp.exp(cs), jnp.broadcast_to(chunk_decay, (SUBLANES, LANES))],
                             axis=0)
    expanded = jnp.dot(exp_in, e_ref[...], precision=HIGHEST, preferred_element_type=F32)
    xscale = expanded[0:CHUNK]
    ecs = expanded[CHUNK:2 * CHUNK]
    cdecay = expanded[2 * CHUNK:2 * CHUNK + 1]

    cs_t = cs.T
    dt_t = dt.T
    li = lax.broadcasted_iota(jnp.int32, (CHUNK, CHUNK), 0)
    si = lax.broadcasted_iota(jnp.int32, (CHUNK, CHUNK), 1)
    causal = li >= si
    lane = lax.broadcasted_iota(jnp.int32, (CHUNK, LANES), 1)

    cbs = []
    for g in range(SSD_GROUPS):
        bm = bc[:, g * SSD_STATE:(g + 1) * SSD_STATE].astype(BF16)
        cm = bc[:, (SSD_GROUPS + g) * SSD_STATE:(SSD_GROUPS + g + 1) * SSD_STATE].astype(BF16)
        cbs.append(lax.dot_general(cm, bm, NT_DIMS, preferred_element_type=F32))
        cols = slice(g * gw, (g + 1) * gw)
        xsc = (xs[:, cols] * xscale[:, cols]).astype(BF16)
        states = lax.dot_general(bm, xsc, TN_DIMS, preferred_element_type=F32)
        h = h_ref[g]
        y_ref[:, cols] = _mm(cm, h.astype(BF16)) * ecs[:, cols]
        h_ref[g] = h * cdecay[:, cols] + states

    for q in range(heads // 2):
        mats = []
        for e in (2 * q, 2 * q + 1):
            diff = cs[:, e:e + 1] - cs_t[e:e + 1, :]
            lmat = jnp.exp(jnp.where(causal, diff, -jnp.inf))
            mats.append(cbs[e // hpg] * lmat * dt_t[e:e + 1, :])
        lhs = jnp.concatenate(mats, axis=0).astype(BF16)
        cols = slice(q * LANES, (q + 1) * LANES)
        res = _mm(lhs, xs_bf[:, cols])
        y_ref[:, cols] += jnp.where(lane < HEAD_DIM, res[0:CHUNK], res[CHUNK:2 * CHUNK])

    z = z_ref[...]
    y = (y_ref[...] + dexp_ref[...] * xs) * (z * _sigmoid(z))
    for g in range(SSD_GROUPS):
        cols = slice(g * gw, (g + 1) * gw)
        o_ref[:, cols] = (_rms(y[:, cols], GATED_NORM_EPS) * ng_ref[:, cols]).astype(o_ref.dtype)


def _ssd_mixer(p_main, p_small, conv_w, conv_b, dt_bias, a_log, d_skip, norm_g, *, d_ssd, col_xs, col_bc, col_dt):
    b, s, _ = p_main.shape
    heads = d_ssd // HEAD_DIM
    d_bc = 2 * SSD_GROUPS * SSD_STATE
    assert s % CHUNK == 0 and heads % (2 * SSD_GROUPS) == 0 and heads <= LANES
    assert col_xs % d_ssd == 0 and col_bc % d_bc == 0 and col_dt % LANES == 0
    rows8 = CHUNK // SUBLANES

    def prev_map(cb):
        return lambda bi, i: (bi, jnp.maximum(i * rows8 - 1, 0), cb)

    def const(shape):
        return pl.BlockSpec(shape, lambda bi, i: (0,) * len(shape))

    pad = LANES - heads
    e_mat = (jnp.arange(LANES)[:, None] == (jnp.arange(d_ssd)[None, :] // HEAD_DIM)).astype(F32)
    tril = jnp.tril(jnp.ones((CHUNK, CHUNK), F32))
    return pl.pallas_call(
        functools.partial(_ssd_kernel, heads=heads),
        grid=(b, s // CHUNK),
        in_specs=[
            pl.BlockSpec((None, CHUNK, d_ssd), lambda bi, i: (bi, i, 0)),
            pl.BlockSpec((None, CHUNK, d_ssd), lambda bi, i: (bi, i, col_xs // d_ssd)),
            pl.BlockSpec((None, CHUNK, d_bc), lambda bi, i: (bi, i, col_bc // d_bc)),
            pl.BlockSpec((None, SUBLANES, d_ssd), prev_map(col_xs // d_ssd)),
            pl.BlockSpec((None, SUBLANES, d_bc), prev_map(col_bc // d_bc)),
            pl.BlockSpec((None, CHUNK, LANES), lambda bi, i: (bi, i, col_dt // LANES)),
            const((CONV_WIDTH, d_ssd)), const((CONV_WIDTH, d_bc)), const((1, d_ssd)), const((1, d_bc)),
            const((1, LANES)), const((1, LANES)), const((1, d_ssd)), const((1, d_ssd)),
            const((LANES, d_ssd)), const((CHUNK, CHUNK)),
        ],
        out_specs=pl.BlockSpec((None, CHUNK, d_ssd), lambda bi, i: (bi, i, 0)),
        out_shape=jax.ShapeDtypeStruct((b, s, d_ssd), BF16),
        scratch_shapes=[pltpu.VMEM((SSD_GROUPS, SSD_STATE, d_ssd // SSD_GROUPS), F32),
                        pltpu.VMEM((CHUNK, d_ssd), F32)],
        compiler_params=_params("parallel", "arbitrary"),
        name="ssd_mixer",
    )(p_main, p_main, p_main, p_main, p_main, p_small,
      conv_w[:, :d_ssd], conv_w[:, d_ssd:], conv_b[None, :d_ssd], conv_b[None, d_ssd:],
      jnp.pad(dt_bias, (0, pad))[None], jnp.pad(a_log, (0, pad))[None],
      jnp.repeat(d_skip, HEAD_DIM)[None], norm_g[None], e_mat, tril)


def _rwkv_prep_kernel(r_ref, k_ref, v_ref, sm_ref, rp_ref, kp_ref, vp_ref, smp_ref,
                      mur_ref, muk_ref, muv_ref, musm_ref, w0_ref, a0_ref, w2_ref, a2_ref, g2_ref,
                      ro_ref, ko_ref, vo_ref, lw_ref, ao_ref, go_ref, *, lora_w, lora_a):
    first = pl.program_id(1) == 0

    def token_shift(cur_ref, prev_ref, mu_ref):
        cur = cur_ref[...]
        prev = _shift_rows(cur, prev_ref[...], 1, first)
        return cur + (prev - cur) * mu_ref[...]

    ro_ref[...] = token_shift(r_ref, rp_ref, mur_ref)
    ko_ref[...] = token_shift(k_ref, kp_ref, muk_ref)
    vo_ref[...] = token_shift(v_ref, vp_ref, muv_ref)
    sm = token_shift(sm_ref, smp_ref, musm_ref)
    wd = sm[:, 0:lora_w]
    ad = sm[:, lora_w:lora_w + lora_a]
    gd = sm[:, lora_w + lora_a:]
    dw = jnp.dot(jnp.tanh(wd), w2_ref[...], precision=HIGHEST, preferred_element_type=F32)
    logw = -_softplus(-(w0_ref[...] + dw)) - 0.5
    lw_ref[...] = -jnp.exp(logw)
    da = jnp.dot(ad, a2_ref[...], precision=HIGHEST, preferred_element_type=F32)
    ao_ref[...] = _sigmoid(a0_ref[...] + da)
    go_ref[...] = _mm(_sigmoid(gd).astype(BF16), g2_ref[...])


def _rwkv_prep(p_main, p_small, mu_r, mu_k, mu_v, mu_sm, w0, a0, w2p, a2p, g2, *, d, col_r, tl):
    b, s, _ = p_main.shape
    lw_, la_, lg_ = w2p.shape[0], a2p.shape[0], g2.shape[0]
    wsm = lw_ + la_ + lg_
    assert col_r % d == 0 and s % tl == 0
    rows8 = tl // SUBLANES
    cr = col_r // d

    def cur(width, cb):
        return pl.BlockSpec((None, tl, width), lambda bi, i: (bi, i, cb))

    def prev(width, cb):
        return pl.BlockSpec((None, SUBLANES, width), lambda bi, i: (bi, jnp.maximum(i * rows8 - 1, 0), cb))

    def const(shape):
        return pl.BlockSpec(shape, lambda bi, i: (0,) * len(shape))

    out = jax.ShapeDtypeStruct((b, s, d), F32)
    return pl.pallas_call(
        functools.partial(_rwkv_prep_kernel, lora_w=lw_, lora_a=la_),
        grid=(b, s // tl),
        in_specs=[cur(d, cr), cur(d, cr + 1), cur(d, cr + 2), cur(wsm, 0),
                  prev(d, cr), prev(d, cr + 1), prev(d, cr + 2), prev(wsm, 0),
                  const((1, d)), const((1, d)), const((1, d)), const((1, wsm)), const((1, d)), const((1, d)),
                  const((lw_, d)), const((la_, d)), const((lg_, d))],
        out_specs=[cur(d, 0)] * 6,
        out_shape=[out] * 6,
        compiler_params=_params("parallel", "parallel"),
        name="rwkv_prep",
    )(p_main, p_main, p_main, p_small, p_main, p_main, p_main, p_small,
      mu_r[None], mu_k[None], mu_v[None], mu_sm[None], w0[None], a0[None], w2p, a2p, g2)


def _lockstep(gens):
    results = [None] * len(gens)
    live = list(range(len(gens)))
    while live:
        for i in list(live):
            try:
                next(gens[i])
            except StopIteration as done:
                results[i] = done.value
                live.remove(i)
    return results


def _rwkv_scan_kernel(r_ref, k_ref, v_ref, lw_ref, a_ref, g_ref, kk_ref, ka_ref, rk_ref, gw_ref, gb_ref,
                      tril_ref, bd_ref, o_ref, st_ref, *, group, chunks, unroll):
    width = group * HEAD_DIM

    @pl.when(pl.program_id(2) == 0)
    def _():
        st_ref[...] = jnp.zeros_like(st_ref)

    row = lax.broadcasted_iota(jnp.int32, (CHUNK, width), 0)
    lane = lax.broadcasted_iota(jnp.int32, (CHUNK, width), 1)
    col = lane % HEAD_DIM
    lane_head = lane // HEAD_DIM
    strict = col < row
    incl = col <= row
    eye = (col == row).astype(F32)
    bd_b = bd_ref[...]
    bd_rows = lax.broadcasted_iota(jnp.int32, (width, width), 0) // HEAD_DIM
    bd_cols = lax.broadcasted_iota(jnp.int32, (width, width), 1) // HEAD_DIM
    bd_mask = bd_rows == bd_cols

    def level_mask(m):
        return ((row // (2 * m)) == (col // (2 * m))) & ((row // m) != (col // m))

    def stack(x):
        return jnp.tile(x, (group, 1)) * bd_b

    def split(x, terms):
        parts = []
        for _ in range(terms):
            p = x.astype(BF16)
            parts.append(p)
            x = x - p.astype(F32)
        return parts

    def head_sum(x):
        n = x.shape[0]
        res = _mm(jnp.concatenate(split(x, 2), axis=0), bd_b)
        return res[0:n] + res[n:]

    def cumsum_rows(x):
        res = _mm(tril_ref[...], jnp.concatenate(split(x, 3), axis=1))
        return res[:, 0:width] + res[:, width:2 * width] + res[:, 2 * width:]

    def local_terms(rows):
        r = r_ref[rows, :]
        k = k_ref[rows, :]
        v = v_ref[rows, :]
        lw = lw_ref[rows, :]
        a = a_ref[rows, :]
        kk = k * kk_ref[...]
        ss = head_sum(kk * kk)
        yield
        kk = kk / jnp.maximum(jnp.sqrt(ss), 1e-12)
        k2 = k * (1.0 + (a - 1.0) * ka_ref[...])
        bonus = head_sum(r * k2 * rk_ref[...]) * v
        yield
        cl = cumsum_rows(lw)
        yield
        e_pos = jnp.exp(cl)
        e_neg = jnp.exp(-cl)
        g_last = e_pos[CHUNK - 1:CHUNK, :]
        rt = r * e_pos
        at_b = (-kk * jnp.exp(cl - lw)).astype(BF16)
        rt_b = rt.astype(BF16)
        bt_b = (kk * a * e_neg).astype(BF16)
        kt_b = (k2 * e_neg).astype(BF16)
        v_b = v.astype(BF16)

        ar = jnp.concatenate([at_b, rt_b], axis=0)
        att_b = lax.dot_general(ar, stack(bt_b), NT_DIMS, preferred_element_type=F32)
        att_k = lax.dot_general(ar, stack(kt_b), NT_DIMS, preferred_element_type=F32)
        yield
        n_ab = jnp.where(strict, att_b[0:CHUNK], 0.0)
        a_ak = jnp.where(strict, att_k[0:CHUNK], 0.0)
        a_rb = jnp.where(incl, att_b[CHUNK:], 0.0).astype(BF16)
        a_rk = jnp.where(incl, att_k[CHUNK:], 0.0)

        tinv = eye + jnp.where(level_mask(1), n_ab, 0.0)
        m = 2
        while m < CHUNK:
            off = jnp.where(level_mask(m), n_ab, 0.0).astype(BF16)
            tb = tinv.astype(BF16)
            mid = _mm(tb, stack(off))
            yield
            tinv = tinv + _mm(mid.astype(BF16), stack(tb))
            yield
            m *= 2
        tb = tinv.astype(BF16)

        xy = _mm(jnp.concatenate([a_ak, a_rk], axis=0).astype(BF16), stack(v_b))
        p_b = _mm(tb, stack(at_b)).astype(BF16)
        yield
        u0_b = _mm(tb, stack(xy[0:CHUNK].astype(BF16))).astype(BF16)
        q_b = (rt + _mm(a_rb, stack(p_b))).astype(BF16)
        mt = lax.dot_general(p_b, bt_b, TN_DIMS, preferred_element_type=F32)
        mt_b = (jnp.where(bd_mask, mt, 0.0) * g_last).astype(BF16)
        yield
        y0 = xy[CHUNK:] + _mm(a_rb, stack(u0_b))
        nfull = lax.dot_general(jnp.concatenate([u0_b, v_b], axis=0), jnp.concatenate([bt_b, kt_b], axis=0),
                                TN_DIMS, preferred_element_type=F32)
        nt = jnp.where(lane_head == 0, nfull[0:HEAD_DIM], 0.0)
        for h in range(1, group):
            nt = nt + jnp.where(lane_head == h, nfull[h * HEAD_DIM:(h + 1) * HEAD_DIM], 0.0)
        return q_b, y0, mt_b, nt * g_last, g_last, bonus

    def body(it, carry):
        base = it * (unroll * CHUNK)
        rows = [pl.ds(pl.multiple_of(base + u * CHUNK, CHUNK), CHUNK) for u in range(unroll)]
        local = _lockstep([local_terms(rw) for rw in rows])
        st = st_ref[...]
        ys = []
        for q_b, y0, mt_b, nt, g_last, _ in local:
            st_b = st.astype(BF16)
            ys.append(lax.dot_general(q_b, stack(st_b), NT_DIMS, preferred_element_type=F32) + y0)
            st = st * g_last + _mm(st_b, mt_b) + nt
        st_ref[...] = st
        ycs = [y - head_sum(y) * (1.0 / HEAD_DIM) for y in ys]
        variances = [head_sum(yc * yc) * (1.0 / HEAD_DIM) for yc in ycs]
        for rw, yc, var, loc in zip(rows, ycs, variances, local):
            yn = yc * lax.rsqrt(var + GN_EPS) * gw_ref[...] + gb_ref[...]
            o_ref[rw, :] = ((yn + loc[5]) * g_ref[rw, :]).astype(o_ref.dtype)
        return carry

    lax.fori_loop(0, chunks // unroll, body, 0)


def _rwkv_scan(r, k, v, lw, a, g, k_k, k_a, r_k, gn_w, gn_b, *, tl, group, unroll):
    b, s, d = r.shape
    width = group * HEAD_DIM
    chunks = tl // CHUNK
    assert width % LANES == 0 and d % width == 0 and s % tl == 0 and tl % CHUNK == 0 and chunks % unroll == 0

    seq = pl.BlockSpec((None, tl, width), lambda bi, p, i: (bi, i, p))
    par = pl.BlockSpec((1, width), lambda bi, p, i: (0, p))

    def const(shape):
        return pl.BlockSpec(shape, lambda bi, p, i: (0,) * len(shape))

    tril = jnp.tril(jnp.ones((CHUNK, CHUNK), BF16))
    blk = jnp.arange(width) // HEAD_DIM
    bd = (blk[:, None] == blk[None, :]).astype(BF16)
    return pl.pallas_call(
        functools.partial(_rwkv_scan_kernel, group=group, chunks=chunks, unroll=unroll),
        grid=(b, d // width, s // tl),
        in_specs=[seq] * 6 + [par] * 5 + [const((CHUNK, CHUNK)), const((width, width))],
        out_specs=seq,
        out_shape=jax.ShapeDtypeStruct((b, s, d), BF16),
        scratch_shapes=[pltpu.VMEM((HEAD_DIM, width), F32)],
        compiler_params=_params("parallel", "parallel", "arbitrary"),
        name="rwkv_scan",
    )(r, k, v, lw, a, g, k_k[None], k_a[None], r_k[None], gn_w[None], gn_b[None], tril, bd)


def _pad_cols(w, n):
    return jnp.pad(w, ((0, 0), (0, n - w.shape[1])))


def _pad_rows(w, n):
    return jnp.pad(w, ((0, n - w.shape[0]), (0, 0)))


def _pad_vec(v, n):
    return jnp.pad(v, (0, n - v.shape[0]))


def _round_up(n, m):
    return -(-n // m) * m


def _layer(h, norm1_g, w_in, conv_w, conv_b, dt_bias, a_log, d_skip, ssd_norm_g, mu, w0, w2, a0, a2, g2,
           k_k, k_a, r_k, gn_w, gn_b, w_out, norm2_g, w_gate, w_up, w_down, out_g):
    b, s, d = h.shape
    m = b * s
    d_ssd = ssd_norm_g.shape[0]
    d_rwkv = w0.shape[0]
    d_bc = 2 * SSD_GROUPS * SSD_STATE
    heads = d_ssd // HEAD_DIM
    lw_, la_, lg_ = w2.shape[0], a2.shape[0], g2.shape[0]
    lwp, lap, lgp = (_round_up(n, LANES) for n in (lw_, la_, lg_))

    o_xbc = d_ssd
    o_dt = o_xbc + d_ssd + d_bc
    o_r = o_dt + heads
    o_wd = o_r + 3 * d_rwkv
    o_ad = o_wd + lw_
    o_gd = o_ad + la_
    w_main = jnp.concatenate([w_in[:, :d_ssd], w_in[:, o_xbc:o_xbc + d_ssd], w_in[:, o_r:o_wd],
                              w_in[:, o_xbc + d_ssd:o_dt]], axis=1).astype(BF16)
    w_small = jnp.concatenate([_pad_cols(w_in[:, o_wd:o_ad], lwp), _pad_cols(w_in[:, o_ad:o_gd], lap),
                               _pad_cols(w_in[:, o_gd:], lgp), _pad_cols(w_in[:, o_dt:o_r], LANES)],
                              axis=1).astype(BF16)
    col_xs, col_r, col_bc = d_ssd, 2 * d_ssd, 2 * d_ssd + 3 * d_rwkv
    col_dt = lwp + lap + lgp
    mu_sm = jnp.concatenate([_pad_vec(mu[3 * d_rwkv:3 * d_rwkv + lw_], lwp),
                             _pad_vec(mu[3 * d_rwkv + lw_:3 * d_rwkv + lw_ + la_], lap),
                             _pad_vec(mu[3 * d_rwkv + lw_ + la_:], lgp)])

    tm = min(1024, m)
    u = _rmsnorm(h.reshape(m, d), norm1_g, tm=min(512, m))
    p_main = _matmul(u, w_main, tm=tm, tn=1024, name="in_proj_main").reshape(b, s, -1)
    p_small = _matmul(u, w_small, tm=tm, tn=w_small.shape[1], name="in_proj_small").reshape(b, s, -1)

    y_ssd = _ssd_mixer(p_main, p_small, conv_w, conv_b, dt_bias, a_log, d_skip, ssd_norm_g,
                       d_ssd=d_ssd, col_xs=col_xs, col_bc=col_bc, col_dt=col_dt)

    p_lora = p_small[:, :, :col_dt]
    r, k, v, lw, a, g = _rwkv_prep(
        p_main, p_lora, mu[:d_rwkv], mu[d_rwkv:2 * d_rwkv], mu[2 * d_rwkv:3 * d_rwkv], mu_sm, w0, a0,
        _pad_rows(w2, lwp), _pad_rows(a2, lap), _pad_rows(g2, lgp).astype(BF16),
        d=d_rwkv, col_r=col_r, tl=min(256, s))
    scan_tl = min(512, s)
    y_rwkv = _rwkv_scan(r, k, v, lw, a, g, k_k, k_a, r_k, gn_w, gn_b, tl=scan_tl, group=4,
                        unroll=min(4, scan_tl // CHUNK))

    tm2 = min(512, m)
    h1, v2 = _row_matmul([y_ssd.reshape(m, d_ssd), y_rwkv.reshape(m, d_rwkv)], w_out.astype(BF16), h.reshape(m, d),
                         norm2_g, tm=tm2, tn=512, final=False, name="out_proj")
    ff = _gate_up(v2, w_gate.astype(BF16), w_up.astype(BF16), tm=tm, tn=512)
    out = _row_matmul([ff], w_down.astype(BF16), h1, out_g, tm=tm2, tn=512, final=True, name="down_proj")
    return out.reshape(b, s, d)


def kernel(x, norm1_g, w_in, ssd_conv_w, ssd_conv_b, ssd_dt_bias, ssd_A_log, ssd_D, ssd_norm_g, rwkv_mu, rwkv_w0,
           rwkv_w2, rwkv_a0, rwkv_a2, rwkv_g2, rwkv_k_k, rwkv_k_a, rwkv_r_k, rwkv_gn_w, rwkv_gn_b, w_out,
           norm2_g, w_gate, w_up, w_down, norm_f_g):
    assert w_in.shape[0] == 1, "single-layer block"
    return _layer(x, norm1_g[0], w_in[0], ssd_conv_w[0], ssd_conv_b[0], ssd_dt_bias[0], ssd_A_log[0], ssd_D[0],
                  ssd_norm_g[0], rwkv_mu[0], rwkv_w0[0], rwkv_w2[0], rwkv_a0[0], rwkv_a2[0], rwkv_g2[0],
                  rwkv_k_k[0], rwkv_k_a[0], rwkv_r_k[0], rwkv_gn_w[0], rwkv_gn_b[0], w_out[0], norm2_g[0],
                  w_gate[0], w_up[0], w_down[0], norm_f_g)
```

```python
import functools

import jax
import jax.numpy as jnp
from jax import lax
from jax.experimental import pallas as pl
from jax.experimental.pallas import tpu as pltpu

F32 = jnp.float32
BF16 = jnp.bfloat16
HIGHEST = lax.Precision.HIGHEST

HEAD_DIM = 64
CHUNK = 64
SSD_GROUPS = 4
SSD_STATE = 128
CONV_WIDTH = 4
RMS_EPS = 1e-6
GATED_NORM_EPS = 1e-5
GN_EPS = 64e-5

LANES = 128
SUBLANES = 8
BF16_ROWS = 2 * SUBLANES
VMEM_LIMIT_BYTES = 56 * 1024 * 1024

NT_DIMS = (((1,), (1,)), ((), ()))
TN_DIMS = (((0,), (0,)), ((), ()))


def _sigmoid(x):
    return jax.nn.sigmoid(x)


def _softplus(x):
    return jnp.maximum(x, 0.0) + jnp.log1p(jnp.exp(-jnp.abs(x)))


def _rms(h, eps):
    return h * lax.rsqrt(jnp.mean(h * h, axis=-1, keepdims=True) + eps)


def _params(*sem):
    return pltpu.CompilerParams(dimension_semantics=sem, vmem_limit_bytes=VMEM_LIMIT_BYTES)


def _mm(a, b):
    return jnp.dot(a, b, preferred_element_type=F32)


def _rmsnorm_kernel(x_ref, g_ref, o_ref):
    o_ref[...] = (_rms(x_ref[...], RMS_EPS) * g_ref[...]).astype(o_ref.dtype)


def _rmsnorm(x, g, tm):
    m, d = x.shape
    return pl.pallas_call(
        _rmsnorm_kernel,
        grid=(m // tm,),
        in_specs=[pl.BlockSpec((tm, d), lambda i: (i, 0)), pl.BlockSpec((1, d), lambda i: (0, 0))],
        out_specs=pl.BlockSpec((tm, d), lambda i: (i, 0)),
        out_shape=jax.ShapeDtypeStruct((m, d), BF16),
        compiler_params=_params("parallel"),
        name="rmsnorm",
    )(x, g.reshape(1, d))


def _matmul_kernel(a_ref, w_ref, o_ref):
    o_ref[...] = _mm(a_ref[...], w_ref[...]).astype(o_ref.dtype)


def _matmul(a, w, *, tm, tn, name):
    m, kd = a.shape
    n = w.shape[1]
    return pl.pallas_call(
        _matmul_kernel,
        grid=(m // tm, n // tn),
        in_specs=[pl.BlockSpec((tm, kd), lambda i, j: (i, 0)), pl.BlockSpec((kd, tn), lambda i, j: (0, j))],
        out_specs=pl.BlockSpec((tm, tn), lambda i, j: (i, j)),
        out_shape=jax.ShapeDtypeStruct((m, n), F32),
        compiler_params=_params("parallel", "parallel"),
        name=name,
    )(a, w)


def _row_matmul_kernel(*refs, n_in, nj, tn, final):
    a_refs, w_refs = refs[:n_in], refs[n_in:2 * n_in]
    res_ref, g_ref, h_ref = refs[2 * n_in:2 * n_in + 3]
    j = pl.program_id(1)
    part = _mm(a_refs[0][...], w_refs[0][...])
    for a_ref, w_ref in zip(a_refs[1:], w_refs[1:]):
        part = part + _mm(a_ref[...], w_ref[...])
    for jj in range(nj):
        @pl.when(j == jj)
        def _(jj=jj):
            cols = slice(jj * tn, (jj + 1) * tn)
            h_ref[:, cols] = part + res_ref[:, cols]

    @pl.when(j == nj - 1)
    def _():
        normed = _rms(h_ref[...], RMS_EPS) * g_ref[...]
        if final:
            h_ref[...] = normed
        else:
            refs[2 * n_in + 3][...] = normed.astype(BF16)


def _row_matmul(a_list, w, res, norm_g, *, tm, tn, final, name):
    m, n = res.shape
    kds = [a.shape[1] for a in a_list]
    assert sum(kds) == w.shape[0] and all(kd == kds[0] for kd in kds) and n % tn == 0
    nj = n // tn
    row = pl.BlockSpec((tm, n), lambda i, j: (i, 0))
    in_specs = [pl.BlockSpec((tm, kd), lambda i, j: (i, 0)) for kd in kds]
    in_specs += [pl.BlockSpec((kds[0], tn), lambda i, j, idx=idx: (idx, j)) for idx in range(len(kds))]
    in_specs += [row, pl.BlockSpec((1, n), lambda i, j: (0, 0))]
    h_shape = jax.ShapeDtypeStruct((m, n), F32)
    return pl.pallas_call(
        functools.partial(_row_matmul_kernel, n_in=len(kds), nj=nj, tn=tn, final=final),
        grid=(m // tm, nj),
        in_specs=in_specs,
        out_specs=row if final else [row, row],
        out_shape=h_shape if final else [h_shape, jax.ShapeDtypeStruct((m, n), BF16)],
        compiler_params=_params("parallel", "arbitrary"),
        name=name,
    )(*a_list, *([w] * len(kds)), res, norm_g.reshape(1, n))


def _gate_up_kernel(a_ref, wg_ref, wu_ref, o_ref):
    a = a_ref[...]
    gate = _mm(a, wg_ref[...])
    up = _mm(a, wu_ref[...])
    o_ref[...] = (gate * _sigmoid(gate) * up).astype(o_ref.dtype)


def _gate_up(a, wg, wu, *, tm, tn):
    m, kd = a.shape
    n = wg.shape[1]
    return pl.pallas_call(
        _gate_up_kernel,
        grid=(m // tm, n // tn),
        in_specs=[pl.BlockSpec((tm, kd), lambda i, j: (i, 0)),
                  pl.BlockSpec((kd, tn), lambda i, j: (0, j)),
                  pl.BlockSpec((kd, tn), lambda i, j: (0, j))],
        out_specs=pl.BlockSpec((tm, tn), lambda i, j: (i, j)),
        out_shape=jax.ShapeDtypeStruct((m, n), BF16),
        compiler_params=_params("parallel", "parallel"),
        name="gate_up",
    )(a, wg, wu)


def _shift_rows(cur, prev8, d, first):
    rolled = pltpu.roll(cur, d, axis=0)
    prolled = jnp.where(first, 0.0, pltpu.roll(prev8, d, axis=0))
    row = lax.broadcasted_iota(jnp.int32, prolled.shape, 0)
    top = jnp.where(row < d, prolled, rolled[0:SUBLANES])
    return jnp.concatenate([top, rolled[SUBLANES:]], axis=0)


def _ssd_kernel(z_ref, xs_ref, bc_ref, xsp_ref, bcp_ref, dt_ref, cwx_ref, cwbc_ref, cbx_ref, cbbc_ref,
                dtb_ref, alog_ref, dexp_ref, ng_ref, e_ref, tril_ref, o_ref, h_ref, y_ref, *, heads):
    first = pl.program_id(1) == 0
    hpg = heads // SSD_GROUPS
    gw = hpg * HEAD_DIM

    @pl.when(first)
    def _():
        h_ref[...] = jnp.zeros_like(h_ref)

    def conv_silu(cur, prev8, w_ref, b_ref):
        acc = cur * w_ref[CONV_WIDTH - 1:CONV_WIDTH, :] + b_ref[...]
        for d in range(1, CONV_WIDTH):
            acc = acc + _shift_rows(cur, prev8, d, first) * w_ref[CONV_WIDTH - 1 - d:CONV_WIDTH - d, :]
        return acc * _sigmoid(acc)

    xs = conv_silu(xs_ref[...], xsp_ref[...], cwx_ref, cbx_ref)
    bc = conv_silu(bc_ref[...], bcp_ref[...], cwbc_ref, cbbc_ref)
    xs_bf = xs.astype(BF16)

    dt = _softplus(dt_ref[...] + dtb_ref[...])
    adt = dt * (-jnp.exp(alog_ref[...]))
    cs = jnp.dot(tril_ref[...], adt, precision=HIGHEST, preferred_element_type=F32)
    cs_last = cs[CHUNK - 1:CHUNK, :]
    decay_states = jnp.exp(cs_last - cs)
    chunk_decay = jnp.exp(cs_last)
    exp_in = jnp.concatenate([dt * decay_states, jnp.exp(cs), jnp.broadcast_to(chunk_decay, (BF16_ROWS, LANES))],
                             axis=0)
    exp_hi = exp_in.astype(BF16)
    exp_lo = (exp_in - exp_hi.astype(F32)).astype(BF16)
    expanded = _mm(jnp.concatenate([exp_hi, exp_lo], axis=0), e_ref[...])
    expanded = expanded[0:exp_in.shape[0]] + expanded[exp_in.shape[0]:]
    xscale = expanded[0:CHUNK]
    ecs = expanded[CHUNK:2 * CHUNK]
    cdecay = expanded[2 * CHUNK:2 * CHUNK + 1]

    cs_t = cs.T
    dt_t = dt.T
    li = lax.broadcasted_iota(jnp.int32, (CHUNK, CHUNK), 0)
    si = lax.broadcasted_iota(jnp.int32, (CHUNK, CHUNK), 1)
    causal = li >= si
    lane = lax.broadcasted_iota(jnp.int32, (CHUNK, LANES), 1)

    cbs = []
    for g in range(SSD_GROUPS):
        bm = bc[:, g * SSD_STATE:(g + 1) * SSD_STATE].astype(BF16)
        cm = bc[:, (SSD_GROUPS + g) * SSD_STATE:(SSD_GROUPS + g + 1) * SSD_STATE].astype(BF16)
        cbs.append(lax.dot_general(cm, bm, NT_DIMS, preferred_element_type=F32))
        cols = slice(g * gw, (g + 1) * gw)
        xsc = (xs[:, cols] * xscale[:, cols]).astype(BF16)
        states = lax.dot_general(bm, xsc, TN_DIMS, preferred_element_type=F32)
        h = h_ref[g]
        y_ref[:, cols] = _mm(cm, h.astype(BF16)) * ecs[:, cols]
        h_ref[g] = h * cdecay[:, cols] + states

    for q in range(heads // 2):
        mats = []
        for e in (2 * q, 2 * q + 1):
            diff = cs[:, e:e + 1] - cs_t[e:e + 1, :]
            lmat = jnp.exp(jnp.where(causal, diff, -jnp.inf))
            mats.append(cbs[e // hpg] * lmat * dt_t[e:e + 1, :])
        lhs = jnp.concatenate(mats, axis=0).astype(BF16)
        cols = slice(q * LANES, (q + 1) * LANES)
        res = _mm(lhs, xs_bf[:, cols])
        y_ref[:, cols] += jnp.where(lane < HEAD_DIM, res[0:CHUNK], res[CHUNK:2 * CHUNK])

    z = z_ref[...]
    y = (y_ref[...] + dexp_ref[...] * xs) * (z * _sigmoid(z))
    for g in range(SSD_GROUPS):
        cols = slice(g * gw, (g + 1) * gw)
        o_ref[:, cols] = (_rms(y[:, cols], GATED_NORM_EPS) * ng_ref[:, cols]).astype(o_ref.dtype)


def _ssd_mixer(p_main, p_small, conv_w, conv_b, dt_bias, a_log, d_skip, norm_g, *, d_ssd, col_xs, col_bc, col_dt):
    b, s, _ = p_main.shape
    heads = d_ssd // HEAD_DIM
    d_bc = 2 * SSD_GROUPS * SSD_STATE
    assert s % CHUNK == 0 and heads % (2 * SSD_GROUPS) == 0 and heads <= LANES
    assert col_xs % d_ssd == 0 and col_bc % d_bc == 0 and col_dt % LANES == 0
    rows8 = CHUNK // SUBLANES

    def prev_map(cb):
        return lambda bi, i: (bi, jnp.maximum(i * rows8 - 1, 0), cb)

    def const(shape):
        return pl.BlockSpec(shape, lambda bi, i: (0,) * len(shape))

    pad = LANES - heads
    e_mat = (jnp.arange(LANES)[:, None] == (jnp.arange(d_ssd)[None, :] // HEAD_DIM)).astype(BF16)
    tril = jnp.tril(jnp.ones((CHUNK, CHUNK), F32))
    return pl.pallas_call(
        functools.partial(_ssd_kernel, heads=heads),
        grid=(b, s // CHUNK),
        in_specs=[
            pl.BlockSpec((None, CHUNK, d_ssd), lambda bi, i: (bi, i, 0)),
            pl.BlockSpec((None, CHUNK, d_ssd), lambda bi, i: (bi, i, col_xs // d_ssd)),
            pl.BlockSpec((None, CHUNK, d_bc), lambda bi, i: (bi, i, col_bc // d_bc)),
            pl.BlockSpec((None, SUBLANES, d_ssd), prev_map(col_xs // d_ssd)),
            pl.BlockSpec((None, SUBLANES, d_bc), prev_map(col_bc // d_bc)),
            pl.BlockSpec((None, CHUNK, LANES), lambda bi, i: (bi, i, col_dt // LANES)),
            const((CONV_WIDTH, d_ssd)), const((CONV_WIDTH, d_bc)), const((1, d_ssd)), const((1, d_bc)),
            const((1, LANES)), const((1, LANES)), const((1, d_ssd)), const((1, d_ssd)),
            const((LANES, d_ssd)), const((CHUNK, CHUNK)),
        ],
        out_specs=pl.BlockSpec((None, CHUNK, d_ssd), lambda bi, i: (bi, i, 0)),
        out_shape=jax.ShapeDtypeStruct((b, s, d_ssd), BF16),
        scratch_shapes=[pltpu.VMEM((SSD_GROUPS, SSD_STATE, d_ssd // SSD_GROUPS), F32),
                        pltpu.VMEM((CHUNK, d_ssd), F32)],
        compiler_params=_params("parallel", "arbitrary"),
        name="ssd_mixer",
    )(p_main, p_main, p_main, p_main, p_main, p_small,
      conv_w[:, :d_ssd], conv_w[:, d_ssd:], conv_b[None, :d_ssd], conv_b[None, d_ssd:],
      jnp.pad(dt_bias, (0, pad))[None], jnp.pad(a_log, (0, pad))[None],
      jnp.repeat(d_skip, HEAD_DIM)[None], norm_g[None], e_mat, tril)


def _split_bf16(x, terms):
    parts = []
    for _ in range(terms):
        p = x.astype(BF16)
        parts.append(p)
        x = x - p.astype(F32)
    return parts


def _mm_split(x, w_hi, w_lo):
    x_hi, x_lo = _split_bf16(x, 2)
    n = x.shape[0]
    first = _mm(jnp.concatenate([x_hi, x_lo], axis=0), w_hi)
    return first[0:n] + first[n:] + _mm(x_hi, w_lo)


def _lockstep(gens):
    results = [None] * len(gens)
    live = list(range(len(gens)))
    while live:
        for i in list(live):
            try:
                next(gens[i])
            except StopIteration as done:
                results[i] = done.value
                live.remove(i)
    return results


def _rwkv_scan_kernel(rin_ref, kin_ref, vin_ref, rp_ref, kp_ref, vp_ref, wd_ref, ad_ref, gd_ref,
                      wdp_ref, adp_ref, gdp_ref, mur_ref, muk_ref, muv_ref, muwd_ref, muad_ref, mugd_ref,
                      w0_ref, a0_ref, w2h_ref, w2l_ref, a2h_ref, a2l_ref, g2_ref,
                      kk_ref, ka_ref, rk_ref, gw_ref, gb_ref, tril_ref, bd_ref, o_ref,
                      st_ref, r_ref, k_ref, v_ref, lw_ref, a_ref, g_ref, *, group, chunks, unroll):
    width = group * HEAD_DIM
    first = pl.program_id(2) == 0

    @pl.when(first)
    def _():
        st_ref[...] = jnp.zeros_like(st_ref)

    def token_shift(cur_ref, prev_ref, mu_ref):
        cur = cur_ref[...]
        prev = _shift_rows(cur, prev_ref[...], 1, first)
        return cur + (prev - cur) * mu_ref[...]

    r_ref[...] = token_shift(rin_ref, rp_ref, mur_ref)
    k_ref[...] = token_shift(kin_ref, kp_ref, muk_ref)
    v_ref[...] = token_shift(vin_ref, vp_ref, muv_ref)
    dw = _mm_split(jnp.tanh(token_shift(wd_ref, wdp_ref, muwd_ref)), w2h_ref[...], w2l_ref[...])
    lw_ref[...] = -jnp.exp(-_softplus(-(w0_ref[...] + dw)) - 0.5)
    da = _mm_split(token_shift(ad_ref, adp_ref, muad_ref), a2h_ref[...], a2l_ref[...])
    a_ref[...] = _sigmoid(a0_ref[...] + da)
    g_ref[...] = _mm(_sigmoid(token_shift(gd_ref, gdp_ref, mugd_ref)).astype(BF16), g2_ref[...])

    row = lax.broadcasted_iota(jnp.int32, (CHUNK, width), 0)
    lane = lax.broadcasted_iota(jnp.int32, (CHUNK, width), 1)
    col = lane % HEAD_DIM
    lane_head = lane // HEAD_DIM
    strict = col < row
    incl = col <= row
    eye = (col == row).astype(F32)
    bd_b = bd_ref[...]
    bd_rows = lax.broadcasted_iota(jnp.int32, (width, width), 0) // HEAD_DIM
    bd_cols = lax.broadcasted_iota(jnp.int32, (width, width), 1) // HEAD_DIM
    bd_mask = bd_rows == bd_cols

    def level_mask(m):
        return ((row // (2 * m)) == (col // (2 * m))) & ((row // m) != (col // m))

    def stack(x):
        return jnp.tile(x, (group, 1)) * bd_b

    def head_sum(x):
        n = x.shape[0]
        res = _mm(jnp.concatenate(_split_bf16(x, 2), axis=0), bd_b)
        return res[0:n] + res[n:]

    def cumsum_rows(x):
        res = _mm(tril_ref[...], jnp.concatenate(_split_bf16(x, 3), axis=1))
        return res[:, 0:width] + res[:, width:2 * width] + res[:, 2 * width:]

    def local_terms(rows):
        r = r_ref[rows, :]
        k = k_ref[rows, :]
        v = v_ref[rows, :]
        lw = lw_ref[rows, :]
        a = a_ref[rows, :]
        kk = k * kk_ref[...]
        ss = head_sum(kk * kk)
        yield
        kk = kk / jnp.maximum(jnp.sqrt(ss), 1e-12)
        k2 = k * (1.0 + (a - 1.0) * ka_ref[...])
        bonus = head_sum(r * k2 * rk_ref[...]) * v
        yield
        cl = cumsum_rows(lw)
        yield
        e_pos = jnp.exp(cl)
        e_neg = jnp.exp(-cl)
        g_last = e_pos[CHUNK - 1:CHUNK, :]
        rt = r * e_pos
        at_b = (-kk * jnp.exp(cl - lw)).astype(BF16)
        rt_b = rt.astype(BF16)
        bt_b = (kk * a * e_neg).astype(BF16)
        kt_b = (k2 * e_neg).astype(BF16)
        v_b = v.astype(BF16)

        ar = jnp.concatenate([at_b, rt_b], axis=0)
        att_b = lax.dot_general(ar, stack(bt_b), NT_DIMS, preferred_element_type=F32)
        att_k = lax.dot_general(ar, stack(kt_b), NT_DIMS, preferred_element_type=F32)
        yield
        n_ab = jnp.where(strict, att_b[0:CHUNK], 0.0)
        a_ak = jnp.where(strict, att_k[0:CHUNK], 0.0)
        a_rb = jnp.where(incl, att_b[CHUNK:], 0.0).astype(BF16)
        a_rk = jnp.where(incl, att_k[CHUNK:], 0.0)

        tinv = eye + jnp.where(level_mask(1), n_ab, 0.0)
        m = 2
        while m < CHUNK:
            off = jnp.where(level_mask(m), n_ab, 0.0).astype(BF16)
            tb = tinv.astype(BF16)
            mid = _mm(tb, stack(off))
            yield
            tinv = tinv + _mm(mid.astype(BF16), stack(tb))
            yield
            m *= 2
        tb = tinv.astype(BF16)

        xy = _mm(jnp.concatenate([a_ak, a_rk], axis=0).astype(BF16), stack(v_b))
        p_b = _mm(tb, stack(at_b)).astype(BF16)
        yield
        u0_b = _mm(tb, stack(xy[0:CHUNK].astype(BF16))).astype(BF16)
        q_b = (rt + _mm(a_rb, stack(p_b))).astype(BF16)
        mt = lax.dot_general(p_b, bt_b, TN_DIMS, preferred_element_type=F32)
        mt_b = (jnp.where(bd_mask, mt, 0.0) * g_last).astype(BF16)
        yield
        y0 = xy[CHUNK:] + _mm(a_rb, stack(u0_b))
        nfull = lax.dot_general(jnp.concatenate([u0_b, v_b], axis=0), jnp.concatenate([bt_b, kt_b], axis=0),
                                TN_DIMS, preferred_element_type=F32)
        nt = jnp.where(lane_head == 0, nfull[0:HEAD_DIM], 0.0)
        for h in range(1, group):
            nt = nt + jnp.where(lane_head == h, nfull[h * HEAD_DIM:(h + 1) * HEAD_DIM], 0.0)
        return q_b, y0, mt_b, nt * g_last, g_last, bonus

    def body(it, carry):
        base = it * (unroll * CHUNK)
        rows = [pl.ds(pl.multiple_of(base + u * CHUNK, CHUNK), CHUNK) for u in range(unroll)]
        local = _lockstep([local_terms(rw) for rw in rows])
        st = st_ref[...]
        ys = []
        for q_b, y0, mt_b, nt, g_last, _ in local:
            st_b = st.astype(BF16)
            ys.append(lax.dot_general(q_b, stack(st_b), NT_DIMS, preferred_element_type=F32) + y0)
            st = st * g_last + _mm(st_b, mt_b) + nt
        st_ref[...] = st
        ycs = [y - head_sum(y) * (1.0 / HEAD_DIM) for y in ys]
        variances = [head_sum(yc * yc) * (1.0 / HEAD_DIM) for yc in ycs]
        for rw, yc, var, loc in zip(rows, ycs, variances, local):
            yn = yc * lax.rsqrt(var + GN_EPS) * gw_ref[...] + gb_ref[...]
            o_ref[rw, :] = ((yn + loc[5]) * g_ref[rw, :]).astype(o_ref.dtype)
        return carry

    lax.fori_loop(0, chunks // unroll, body, 0)


def _rwkv_mixer(p_main, p_small, mu, w0, w2, a0, a2, g2, k_k, k_a, r_k, gn_w, gn_b, *, col_r, tl, group, unroll):
    b, s, _ = p_main.shape
    d = w0.shape[0]
    width = group * HEAD_DIM
    chunks = tl // CHUNK
    lw_, la_, lg_ = w2.shape[0], a2.shape[0], g2.shape[0]
    lwp, lap, lgp = (_round_up(n, LANES) for n in (lw_, la_, lg_))
    assert width % LANES == 0 and d % width == 0 and col_r % width == 0
    assert s % tl == 0 and tl % CHUNK == 0 and chunks % unroll == 0
    assert lwp % lap == 0 and (lwp + lap) % lgp == 0
    rows8 = tl // SUBLANES
    ncol = d // width

    def seq(wd_, cb):
        return pl.BlockSpec((None, tl, wd_), lambda bi, p, i: (bi, i, cb(p)))

    def prev(wd_, cb):
        return pl.BlockSpec((None, SUBLANES, wd_), lambda bi, p, i: (bi, jnp.maximum(i * rows8 - 1, 0), cb(p)))

    def par(rows=1):
        return pl.BlockSpec((rows, width), lambda bi, p, i: (0, p))

    def const(shape):
        return pl.BlockSpec(shape, lambda bi, p, i: (0,) * len(shape))

    rkv_cols = [lambda p, j=j: col_r // width + j * ncol + p for j in range(3)]
    lora_cols = [lambda p: 0, lambda p: lwp // lap, lambda p: (lwp + lap) // lgp]
    lora_w = [lwp, lap, lgp]
    tril = jnp.tril(jnp.ones((CHUNK, CHUNK), BF16))
    blk = jnp.arange(width) // HEAD_DIM
    bd = (blk[:, None] == blk[None, :]).astype(BF16)
    mu_l = mu[3 * d:]
    w2h, w2l = _split_bf16(_pad_rows(w2, lwp), 2)
    a2h, a2l = _split_bf16(_pad_rows(a2, lap), 2)
    frame = pltpu.VMEM((tl, width), F32)
    return pl.pallas_call(
        functools.partial(_rwkv_scan_kernel, group=group, chunks=chunks, unroll=unroll),
        grid=(b, ncol, s // tl),
        in_specs=([seq(width, c) for c in rkv_cols] + [prev(width, c) for c in rkv_cols]
                  + [seq(w, c) for w, c in zip(lora_w, lora_cols)] + [prev(w, c) for w, c in zip(lora_w, lora_cols)]
                  + [par()] * 3 + [const((1, w)) for w in lora_w] + [par()] * 2
                  + [par(lwp)] * 2 + [par(lap)] * 2 + [par(lgp)] + [par()] * 5
                  + [const((CHUNK, CHUNK)), const((width, width))]),
        out_specs=pl.BlockSpec((None, tl, width), lambda bi, p, i: (bi, i, p)),
        out_shape=jax.ShapeDtypeStruct((b, s, d), BF16),
        scratch_shapes=[pltpu.VMEM((HEAD_DIM, width), F32)] + [frame] * 6,
        compiler_params=_params("parallel", "parallel", "arbitrary"),
        name="rwkv_scan",
    )(p_main, p_main, p_main, p_main, p_main, p_main, p_small, p_small, p_small, p_small, p_small, p_small,
      mu[None, :d], mu[None, d:2 * d], mu[None, 2 * d:3 * d],
      _pad_vec(mu_l[:lw_], lwp)[None], _pad_vec(mu_l[lw_:lw_ + la_], lap)[None], _pad_vec(mu_l[lw_ + la_:], lgp)[None],
      w0[None], a0[None], w2h, w2l, a2h, a2l, _pad_rows(g2, lgp).astype(BF16),
      k_k[None], k_a[None], r_k[None], gn_w[None], gn_b[None], tril, bd)


def _pad_cols(w, n):
    return jnp.pad(w, ((0, 0), (0, n - w.shape[1])))


def _pad_rows(w, n):
    return jnp.pad(w, ((0, n - w.shape[0]), (0, 0)))


def _pad_vec(v, n):
    return jnp.pad(v, (0, n - v.shape[0]))


def _round_up(n, m):
    return -(-n // m) * m


def _layer(h, norm1_g, w_in, conv_w, conv_b, dt_bias, a_log, d_skip, ssd_norm_g, mu, w0, w2, a0, a2, g2,
           k_k, k_a, r_k, gn_w, gn_b, w_out, norm2_g, w_gate, w_up, w_down, out_g):
    b, s, d = h.shape
    m = b * s
    d_ssd = ssd_norm_g.shape[0]
    d_rwkv = w0.shape[0]
    d_bc = 2 * SSD_GROUPS * SSD_STATE
    heads = d_ssd // HEAD_DIM
    lw_, la_, lg_ = w2.shape[0], a2.shape[0], g2.shape[0]
    lwp, lap, lgp = (_round_up(n, LANES) for n in (lw_, la_, lg_))

    o_xbc = d_ssd
    o_dt = o_xbc + d_ssd + d_bc
    o_r = o_dt + heads
    o_wd = o_r + 3 * d_rwkv
    o_ad = o_wd + lw_
    o_gd = o_ad + la_
    w_in = w_in.astype(BF16)
    w_main = jnp.concatenate([w_in[:, :d_ssd], w_in[:, o_xbc:o_xbc + d_ssd], w_in[:, o_r:o_wd],
                              w_in[:, o_xbc + d_ssd:o_dt]], axis=1)
    w_small = jnp.concatenate([_pad_cols(w_in[:, o_wd:o_ad], lwp), _pad_cols(w_in[:, o_ad:o_gd], lap),
                               _pad_cols(w_in[:, o_gd:], lgp), _pad_cols(w_in[:, o_dt:o_r], LANES)], axis=1)
    col_xs, col_r, col_bc = d_ssd, 2 * d_ssd, 2 * d_ssd + 3 * d_rwkv
    col_dt = lwp + lap + lgp

    tm = min(1024, m)
    u = _rmsnorm(h.reshape(m, d), norm1_g, tm=min(512, m))
    p_main = _matmul(u, w_main, tm=tm, tn=1024, name="in_proj_main").reshape(b, s, -1)
    p_small = _matmul(u, w_small, tm=tm, tn=w_small.shape[1], name="in_proj_small").reshape(b, s, -1)

    y_ssd = _ssd_mixer(p_main, p_small, conv_w, conv_b, dt_bias, a_log, d_skip, ssd_norm_g,
                       d_ssd=d_ssd, col_xs=col_xs, col_bc=col_bc, col_dt=col_dt)

    scan_tl = min(512, s)
    y_rwkv = _rwkv_mixer(p_main, p_small, mu, w0, w2, a0, a2, g2, k_k, k_a, r_k, gn_w, gn_b,
                         col_r=col_r, tl=scan_tl, group=4, unroll=min(8, scan_tl // CHUNK))

    tm2 = min(512, m)
    h1, v2 = _row_matmul([y_ssd.reshape(m, d_ssd), y_rwkv.reshape(m, d_rwkv)], w_out.astype(BF16), h.reshape(m, d),
                         norm2_g, tm=tm2, tn=512, final=False, name="out_proj")
    ff = _gate_up(v2, w_gate.astype(BF16), w_up.astype(BF16), tm=tm, tn=512)
    out = _row_matmul([ff], w_down.astype(BF16), h1, out_g, tm=tm2, tn=512, final=True, name="down_proj")
    return out.reshape(b, s, d)


def kernel(x, norm1_g, w_in, ssd_conv_w, ssd_conv_b, ssd_dt_bias, ssd_A_log, ssd_D, ssd_norm_g, rwkv_mu, rwkv_w0,
           rwkv_w2, rwkv_a0, rwkv_a2, rwkv_g2, rwkv_k_k, rwkv_k_a, rwkv_r_k, rwkv_gn_w, rwkv_gn_b, w_out,
           norm2_g, w_gate, w_up, w_down, norm_f_g):
    assert w_in.shape[0] == 1, "single-layer block"
    return _layer(x, norm1_g[0], w_in[0], ssd_conv_w[0], ssd_conv_b[0], ssd_dt_bias[0], ssd_A_log[0], ssd_D[0],
                  ssd_norm_g[0], rwkv_mu[0], rwkv_w0[0], rwkv_w2[0], rwkv_a0[0], rwkv_a2[0], rwkv_g2[0],
                  rwkv_k_k[0], rwkv_k_a[0], rwkv_r_k[0], rwkv_gn_w[0], rwkv_gn_b[0], w_out[0], norm2_g[0],
                  w_gate[0], w_up[0], w_down[0], norm_f_g)
```

```python
import functools

import jax
import jax.numpy as jnp
from jax import lax
from jax.experimental import pallas as pl
from jax.experimental.pallas import tpu as pltpu

F32 = jnp.float32
BF16 = jnp.bfloat16
HIGHEST = lax.Precision.HIGHEST

HEAD_DIM = 64
CHUNK = 64
SSD_GROUPS = 4
SSD_STATE = 128
CONV_WIDTH = 4
RMS_EPS = 1e-6
GATED_NORM_EPS = 1e-5
GN_EPS = 64e-5

LANES = 128
SUBLANES = 8
BF16_ROWS = 2 * SUBLANES
VMEM_LIMIT_BYTES = 56 * 1024 * 1024

NT_DIMS = (((1,), (1,)), ((), ()))
TN_DIMS = (((0,), (0,)), ((), ()))


def _sigmoid(x):
    return jax.nn.sigmoid(x)


def _softplus(x):
    return jnp.maximum(x, 0.0) + jnp.log1p(jnp.exp(-jnp.abs(x)))


def _rms(h, eps):
    return h * lax.rsqrt(jnp.mean(h * h, axis=-1, keepdims=True) + eps)


def _params(*sem):
    return pltpu.CompilerParams(dimension_semantics=sem, vmem_limit_bytes=VMEM_LIMIT_BYTES)


def _mm(a, b):
    return jnp.dot(a, b, preferred_element_type=F32)


def _rmsnorm_kernel(x_ref, g_ref, o_ref):
    o_ref[...] = (_rms(x_ref[...], RMS_EPS) * g_ref[...]).astype(o_ref.dtype)


def _rmsnorm(x, g, tm):
    m, d = x.shape
    return pl.pallas_call(
        _rmsnorm_kernel,
        grid=(m // tm,),
        in_specs=[pl.BlockSpec((tm, d), lambda i: (i, 0)), pl.BlockSpec((1, d), lambda i: (0, 0))],
        out_specs=pl.BlockSpec((tm, d), lambda i: (i, 0)),
        out_shape=jax.ShapeDtypeStruct((m, d), BF16),
        compiler_params=_params("parallel"),
        name="rmsnorm",
    )(x, g.reshape(1, d))


def _matmul_kernel(a_ref, w_ref, o_ref):
    o_ref[...] = _mm(a_ref[...], w_ref[...]).astype(o_ref.dtype)


def _matmul(a, w, *, tm, tn, name, n=None):
    m, kd = a.shape
    n = w.shape[1] if n is None else n
    assert n % tn == 0 and n <= w.shape[1]
    return pl.pallas_call(
        _matmul_kernel,
        grid=(m // tm, n // tn),
        in_specs=[pl.BlockSpec((tm, kd), lambda i, j: (i, 0)), pl.BlockSpec((kd, tn), lambda i, j: (0, j))],
        out_specs=pl.BlockSpec((tm, tn), lambda i, j: (i, j)),
        out_shape=jax.ShapeDtypeStruct((m, n), F32),
        compiler_params=_params("parallel", "parallel"),
        name=name,
    )(a, w)


def _row_matmul_kernel(*refs, n_in, nj, tn, final):
    a_refs, w_refs = refs[:n_in], refs[n_in:2 * n_in]
    res_ref, g_ref, h_ref = refs[2 * n_in:2 * n_in + 3]
    j = pl.program_id(1)
    part = _mm(a_refs[0][...], w_refs[0][...])
    for a_ref, w_ref in zip(a_refs[1:], w_refs[1:]):
        part = part + _mm(a_ref[...], w_ref[...])
    for jj in range(nj):
        @pl.when(j == jj)
        def _(jj=jj):
            cols = slice(jj * tn, (jj + 1) * tn)
            h_ref[:, cols] = part + res_ref[:, cols]

    @pl.when(j == nj - 1)
    def _():
        normed = _rms(h_ref[...], RMS_EPS) * g_ref[...]
        if final:
            h_ref[...] = normed
        else:
            refs[2 * n_in + 3][...] = normed.astype(BF16)


def _row_matmul(a_list, w, res, norm_g, *, tm, tn, final, name):
    m, n = res.shape
    kds = [a.shape[1] for a in a_list]
    assert sum(kds) == w.shape[0] and all(kd == kds[0] for kd in kds) and n % tn == 0
    nj = n // tn
    row = pl.BlockSpec((tm, n), lambda i, j: (i, 0))
    in_specs = [pl.BlockSpec((tm, kd), lambda i, j: (i, 0)) for kd in kds]
    in_specs += [pl.BlockSpec((kds[0], tn), lambda i, j, idx=idx: (idx, j)) for idx in range(len(kds))]
    in_specs += [row, pl.BlockSpec((1, n), lambda i, j: (0, 0))]
    h_shape = jax.ShapeDtypeStruct((m, n), F32)
    return pl.pallas_call(
        functools.partial(_row_matmul_kernel, n_in=len(kds), nj=nj, tn=tn, final=final),
        grid=(m // tm, nj),
        in_specs=in_specs,
        out_specs=row if final else [row, row],
        out_shape=h_shape if final else [h_shape, jax.ShapeDtypeStruct((m, n), BF16)],
        compiler_params=_params("parallel", "arbitrary"),
        name=name,
    )(*a_list, *([w] * len(kds)), res, norm_g.reshape(1, n))


def _gate_up_kernel(a_ref, wg_ref, wu_ref, o_ref):
    a = a_ref[...]
    gate = _mm(a, wg_ref[...])
    up = _mm(a, wu_ref[...])
    o_ref[...] = (gate * _sigmoid(gate) * up).astype(o_ref.dtype)


def _gate_up(a, wg, wu, *, tm, tn):
    m, kd = a.shape
    n = wg.shape[1]
    return pl.pallas_call(
        _gate_up_kernel,
        grid=(m // tm, n // tn),
        in_specs=[pl.BlockSpec((tm, kd), lambda i, j: (i, 0)),
                  pl.BlockSpec((kd, tn), lambda i, j: (0, j)),
                  pl.BlockSpec((kd, tn), lambda i, j: (0, j))],
        out_specs=pl.BlockSpec((tm, tn), lambda i, j: (i, j)),
        out_shape=jax.ShapeDtypeStruct((m, n), BF16),
        compiler_params=_params("parallel", "parallel"),
        name="gate_up",
    )(a, wg, wu)


def _shift_rows(cur, prev8, d, first):
    rolled = pltpu.roll(cur, d, axis=0)
    prolled = pltpu.roll(prev8, d, axis=0)
    if first is not None:
        prolled = jnp.where(first, 0.0, prolled)
    row = lax.broadcasted_iota(jnp.int32, prolled.shape, 0)
    top = jnp.where(row < d, prolled, rolled[0:SUBLANES])
    return jnp.concatenate([top, rolled[SUBLANES:]], axis=0)


def _ssd_kernel(z_ref, xs_ref, bc_ref, xsp_ref, bcp_ref, dt_ref, cwx_ref, cwbc_ref, cbx_ref, cbbc_ref,
                dtb_ref, alog_ref, dexp_ref, ng_ref, e_ref, tril_ref, o_ref, h_ref, y_ref, *, heads, chunks):
    first = pl.program_id(1) == 0

    @pl.when(first)
    def _():
        h_ref[...] = jnp.zeros_like(h_ref)

    for c in range(chunks):
        _ssd_chunk(c, first, z_ref, xs_ref, bc_ref, xsp_ref, bcp_ref, dt_ref, cwx_ref, cwbc_ref, cbx_ref, cbbc_ref,
                   dtb_ref, alog_ref, dexp_ref, ng_ref, e_ref, tril_ref, o_ref, h_ref, y_ref, heads)


def _ssd_chunk(c, first, z_ref, xs_ref, bc_ref, xsp_ref, bcp_ref, dt_ref, cwx_ref, cwbc_ref, cbx_ref, cbbc_ref,
               dtb_ref, alog_ref, dexp_ref, ng_ref, e_ref, tril_ref, o_ref, h_ref, y_ref, heads):
    hpg = heads // SSD_GROUPS
    gw = hpg * HEAD_DIM
    rows = slice(c * CHUNK, (c + 1) * CHUNK)

    def conv_silu(cur_ref, prev_ref, w_ref, b_ref):
        cur = cur_ref[rows, :]
        if c == 0:
            prev8, zero_if = prev_ref[...], first
        else:
            prev8, zero_if = cur_ref[c * CHUNK - SUBLANES:c * CHUNK, :], None
        acc = cur * w_ref[CONV_WIDTH - 1:CONV_WIDTH, :] + b_ref[...]
        for d in range(1, CONV_WIDTH):
            acc = acc + _shift_rows(cur, prev8, d, zero_if) * w_ref[CONV_WIDTH - 1 - d:CONV_WIDTH - d, :]
        return acc * _sigmoid(acc)

    xs = conv_silu(xs_ref, xsp_ref, cwx_ref, cbx_ref)
    bc = conv_silu(bc_ref, bcp_ref, cwbc_ref, cbbc_ref)
    xs_bf = xs.astype(BF16)

    dt = _softplus(dt_ref[rows, :] + dtb_ref[...])
    adt = dt * (-jnp.exp(alog_ref[...]))
    cs = jnp.dot(tril_ref[...], adt, precision=HIGHEST, preferred_element_type=F32)
    cs_last = cs[CHUNK - 1:CHUNK, :]
    decay_states = jnp.exp(cs_last - cs)
    chunk_decay = jnp.exp(cs_last)
    exp_in = jnp.concatenate([dt * decay_states, jnp.exp(cs), jnp.broadcast_to(chunk_decay, (BF16_ROWS, LANES))],
                             axis=0)
    exp_hi = exp_in.astype(BF16)
    exp_lo = (exp_in - exp_hi.astype(F32)).astype(BF16)
    expanded = _mm(jnp.concatenate([exp_hi, exp_lo], axis=0), e_ref[...])
    expanded = expanded[0:exp_in.shape[0]] + expanded[exp_in.shape[0]:]
    xscale = expanded[0:CHUNK]
    ecs = expanded[CHUNK:2 * CHUNK]
    cdecay = expanded[2 * CHUNK:2 * CHUNK + 1]

    cs_t = cs.T
    dt_t = dt.T
    li = lax.broadcasted_iota(jnp.int32, (CHUNK, CHUNK), 0)
    si = lax.broadcasted_iota(jnp.int32, (CHUNK, CHUNK), 1)
    causal = li >= si
    lane = lax.broadcasted_iota(jnp.int32, (CHUNK, LANES), 1)

    cbs = []
    for g in range(SSD_GROUPS):
        bm = bc[:, g * SSD_STATE:(g + 1) * SSD_STATE].astype(BF16)
        cm = bc[:, (SSD_GROUPS + g) * SSD_STATE:(SSD_GROUPS + g + 1) * SSD_STATE].astype(BF16)
        cbs.append(lax.dot_general(cm, bm, NT_DIMS, preferred_element_type=F32))
        cols = slice(g * gw, (g + 1) * gw)
        xsc = (xs[:, cols] * xscale[:, cols]).astype(BF16)
        states = lax.dot_general(bm, xsc, TN_DIMS, preferred_element_type=F32)
        h = h_ref[g]
        y_ref[rows, cols] = _mm(cm, h.astype(BF16)) * ecs[:, cols]
        h_ref[g] = h * cdecay[:, cols] + states

    for q in range(heads // 2):
        mats = []
        for e in (2 * q, 2 * q + 1):
            diff = cs[:, e:e + 1] - cs_t[e:e + 1, :]
            lmat = jnp.exp(jnp.where(causal, diff, -jnp.inf))
            mats.append(cbs[e // hpg] * lmat * dt_t[e:e + 1, :])
        lhs = jnp.concatenate(mats, axis=0).astype(BF16)
        cols = slice(q * LANES, (q + 1) * LANES)
        res = _mm(lhs, xs_bf[:, cols])
        y_ref[rows, cols] += jnp.where(lane < HEAD_DIM, res[0:CHUNK], res[CHUNK:2 * CHUNK])

    z = z_ref[rows, :]
    y = (y_ref[rows, :] + dexp_ref[...] * xs) * (z * _sigmoid(z))
    for g in range(SSD_GROUPS):
        cols = slice(g * gw, (g + 1) * gw)
        o_ref[rows, cols] = (_rms(y[:, cols], GATED_NORM_EPS) * ng_ref[:, cols]).astype(o_ref.dtype)


def _ssd_mixer(p_main, p_small, conv_w, conv_b, dt_bias, a_log, d_skip, norm_g, *, d_ssd, col_xs, col_bc, col_dt,
               tl):
    b, s, _ = p_main.shape
    heads = d_ssd // HEAD_DIM
    d_bc = 2 * SSD_GROUPS * SSD_STATE
    assert s % tl == 0 and tl % CHUNK == 0 and heads % (2 * SSD_GROUPS) == 0 and heads <= LANES
    assert col_xs % d_ssd == 0 and col_bc % d_bc == 0 and col_dt % LANES == 0
    rows8 = tl // SUBLANES

    def prev_map(cb):
        return lambda bi, i: (bi, jnp.maximum(i * rows8 - 1, 0), cb)

    def const(shape):
        return pl.BlockSpec(shape, lambda bi, i: (0,) * len(shape))

    pad = LANES - heads
    e_mat = (jnp.arange(LANES)[:, None] == (jnp.arange(d_ssd)[None, :] // HEAD_DIM)).astype(BF16)
    tril = jnp.tril(jnp.ones((CHUNK, CHUNK), F32))
    return pl.pallas_call(
        functools.partial(_ssd_kernel, heads=heads, chunks=tl // CHUNK),
        grid=(b, s // tl),
        in_specs=[
            pl.BlockSpec((None, tl, d_ssd), lambda bi, i: (bi, i, 0)),
            pl.BlockSpec((None, tl, d_ssd), lambda bi, i: (bi, i, col_xs // d_ssd)),
            pl.BlockSpec((None, tl, d_bc), lambda bi, i: (bi, i, col_bc // d_bc)),
            pl.BlockSpec((None, SUBLANES, d_ssd), prev_map(col_xs // d_ssd)),
            pl.BlockSpec((None, SUBLANES, d_bc), prev_map(col_bc // d_bc)),
            pl.BlockSpec((None, tl, LANES), lambda bi, i: (bi, i, col_dt // LANES)),
            const((CONV_WIDTH, d_ssd)), const((CONV_WIDTH, d_bc)), const((1, d_ssd)), const((1, d_bc)),
            const((1, LANES)), const((1, LANES)), const((1, d_ssd)), const((1, d_ssd)),
            const((LANES, d_ssd)), const((CHUNK, CHUNK)),
        ],
        out_specs=pl.BlockSpec((None, tl, d_ssd), lambda bi, i: (bi, i, 0)),
        out_shape=jax.ShapeDtypeStruct((b, s, d_ssd), BF16),
        scratch_shapes=[pltpu.VMEM((SSD_GROUPS, SSD_STATE, d_ssd // SSD_GROUPS), F32),
                        pltpu.VMEM((tl, d_ssd), F32)],
        compiler_params=_params("parallel", "arbitrary"),
        name="ssd_mixer",
    )(p_main, p_main, p_main, p_main, p_main, p_small,
      conv_w[:, :d_ssd], conv_w[:, d_ssd:], conv_b[None, :d_ssd], conv_b[None, d_ssd:],
      jnp.pad(dt_bias, (0, pad))[None], jnp.pad(a_log, (0, pad))[None],
      jnp.repeat(d_skip, HEAD_DIM)[None], norm_g[None], e_mat, tril)


def _split_bf16(x, terms):
    parts = []
    for _ in range(terms):
        p = x.astype(BF16)
        parts.append(p)
        x = x - p.astype(F32)
    return parts


def _mm_split(x, w_hi, w_lo):
    x_hi, x_lo = _split_bf16(x, 2)
    n = x.shape[0]
    first = _mm(jnp.concatenate([x_hi, x_lo], axis=0), w_hi)
    return first[0:n] + first[n:] + _mm(x_hi, w_lo)


def _lockstep(gens):
    results = [None] * len(gens)
    live = list(range(len(gens)))
    while live:
        for i in list(live):
            try:
                next(gens[i])
            except StopIteration as done:
                results[i] = done.value
                live.remove(i)
    return results


def _rwkv_scan_kernel(rin_ref, kin_ref, vin_ref, rp_ref, kp_ref, vp_ref, wd_ref, ad_ref, gd_ref,
                      wdp_ref, adp_ref, gdp_ref, mur_ref, muk_ref, muv_ref, muwd_ref, muad_ref, mugd_ref,
                      w0_ref, a0_ref, w2h_ref, w2l_ref, a2h_ref, a2l_ref, g2_ref,
                      kk_ref, ka_ref, rk_ref, gw_ref, gb_ref, tril_ref, bd_ref, o_ref,
                      st_ref, *, group, chunks):
    width = group * HEAD_DIM
    first = pl.program_id(2) == 0

    @pl.when(first)
    def _():
        st_ref[...] = jnp.zeros_like(st_ref)

    def token_shift(cur_ref, prev_ref, mu_ref, c):
        cur = cur_ref[c * CHUNK:(c + 1) * CHUNK, :]
        if c == 0:
            prev = _shift_rows(cur, prev_ref[...], 1, first)
        else:
            prev = _shift_rows(cur, cur_ref[c * CHUNK - SUBLANES:c * CHUNK, :], 1, None)
        return cur + (prev - cur) * mu_ref[...]

    row = lax.broadcasted_iota(jnp.int32, (CHUNK, width), 0)
    lane = lax.broadcasted_iota(jnp.int32, (CHUNK, width), 1)
    col = lane % HEAD_DIM
    lane_head = lane // HEAD_DIM
    strict = col < row
    incl = col <= row
    eye = (col == row).astype(F32)
    bd_b = bd_ref[...]
    bd_rows = lax.broadcasted_iota(jnp.int32, (width, width), 0) // HEAD_DIM
    bd_cols = lax.broadcasted_iota(jnp.int32, (width, width), 1) // HEAD_DIM
    bd_mask = bd_rows == bd_cols

    def level_mask(m):
        return ((row // (2 * m)) == (col // (2 * m))) & ((row // m) != (col // m))

    heads_per_tile = LANES // HEAD_DIM
    half = lax.broadcasted_iota(jnp.int32, (1, LANES), 1) // HEAD_DIM
    half_masks = [(half == j).astype(BF16) for j in range(heads_per_tile)]
    zero_tile = jnp.zeros((CHUNK, LANES), BF16)

    def stack(x):
        blocks = []
        for h in range(group):
            t = h // heads_per_tile
            kept = x[:, t * LANES:(t + 1) * LANES] * half_masks[h % heads_per_tile]
            blocks.append(jnp.concatenate([kept if tt == t else zero_tile for tt in range(width // LANES)], axis=1))
        return jnp.concatenate(blocks, axis=0)

    def head_sum(x):
        n = x.shape[0]
        res = _mm(jnp.concatenate(_split_bf16(x, 2), axis=0), bd_b)
        return res[0:n] + res[n:]

    def cumsum_rows(x):
        res = _mm(tril_ref[...], jnp.concatenate(_split_bf16(x, 3), axis=1))
        return res[:, 0:width] + res[:, width:2 * width] + res[:, 2 * width:]

    def local_terms(c):
        r = token_shift(rin_ref, rp_ref, mur_ref, c)
        k = token_shift(kin_ref, kp_ref, muk_ref, c)
        v = token_shift(vin_ref, vp_ref, muv_ref, c)
        dw = _mm_split(jnp.tanh(token_shift(wd_ref, wdp_ref, muwd_ref, c)), w2h_ref[...], w2l_ref[...])
        lw = -jnp.exp(-_softplus(-(w0_ref[...] + dw)) - 0.5)
        yield
        da = _mm_split(token_shift(ad_ref, adp_ref, muad_ref, c), a2h_ref[...], a2l_ref[...])
        a = _sigmoid(a0_ref[...] + da)
        gate = _mm(_sigmoid(token_shift(gd_ref, gdp_ref, mugd_ref, c)).astype(BF16), g2_ref[...])
        yield
        kk = k * kk_ref[...]
        ss = head_sum(kk * kk)
        yield
        kk = kk / jnp.maximum(jnp.sqrt(ss), 1e-12)
        k2 = k * (1.0 + (a - 1.0) * ka_ref[...])
        bonus = head_sum(r * k2 * rk_ref[...]) * v
        yield
        cl = cumsum_rows(lw)
        yield
        e_pos = jnp.exp(cl)
        e_neg = jnp.exp(-cl)
        g_last = e_pos[CHUNK - 1:CHUNK, :]
        rt = r * e_pos
        at_b = (-kk * jnp.exp(cl - lw)).astype(BF16)
        rt_b = rt.astype(BF16)
        bt_b = (kk * a * e_neg).astype(BF16)
        kt_b = (k2 * e_neg).astype(BF16)
        v_b = v.astype(BF16)

        ar = jnp.concatenate([at_b, rt_b], axis=0)
        att_b = lax.dot_general(ar, stack(bt_b), NT_DIMS, preferred_element_type=F32)
        att_k = lax.dot_general(ar, stack(kt_b), NT_DIMS, preferred_element_type=F32)
        yield
        n_ab = jnp.where(strict, att_b[0:CHUNK], 0.0)
        a_ak = jnp.where(strict, att_k[0:CHUNK], 0.0)
        a_rb = jnp.where(incl, att_b[CHUNK:], 0.0).astype(BF16)
        a_rk = jnp.where(incl, att_k[CHUNK:], 0.0)

        tinv = eye + jnp.where(level_mask(1), n_ab, 0.0)
        m = 2
        while m < CHUNK:
            off = jnp.where(level_mask(m), n_ab, 0.0).astype(BF16)
            tb = tinv.astype(BF16)
            mid = _mm(tb, stack(off))
            yield
            tinv = tinv + _mm(mid.astype(BF16), stack(tb))
            yield
            m *= 2
        tb = tinv.astype(BF16)

        xy = _mm(jnp.concatenate([a_ak, a_rk], axis=0).astype(BF16), stack(v_b))
        p_b = _mm(tb, stack(at_b)).astype(BF16)
        yield
        u0_b = _mm(tb, stack(xy[0:CHUNK].astype(BF16))).astype(BF16)
        q_b = (rt + _mm(a_rb, stack(p_b))).astype(BF16)
        mt = lax.dot_general(p_b, bt_b, TN_DIMS, preferred_element_type=F32)
        mt_b = (jnp.where(bd_mask, mt, 0.0) * g_last).astype(BF16)
        yield
        y0 = xy[CHUNK:] + _mm(a_rb, stack(u0_b))
        nfull = lax.dot_general(jnp.concatenate([u0_b, v_b], axis=0), jnp.concatenate([bt_b, kt_b], axis=0),
                                TN_DIMS, preferred_element_type=F32)
        nt = jnp.where(lane_head == 0, nfull[0:HEAD_DIM], 0.0)
        for h in range(1, group):
            nt = nt + jnp.where(lane_head == h, nfull[h * HEAD_DIM:(h + 1) * HEAD_DIM], 0.0)
        return q_b, y0, mt_b, nt * g_last, g_last, bonus, gate

    local = _lockstep([local_terms(c) for c in range(chunks)])
    st = st_ref[...]
    ys = []
    for q_b, y0, mt_b, nt, g_last, _, _ in local:
        st_b = st.astype(BF16)
        ys.append(lax.dot_general(q_b, stack(st_b), NT_DIMS, preferred_element_type=F32) + y0)
        st = st * g_last + _mm(st_b, mt_b) + nt
    st_ref[...] = st
    ycs = [y - head_sum(y) * (1.0 / HEAD_DIM) for y in ys]
    variances = [head_sum(yc * yc) * (1.0 / HEAD_DIM) for yc in ycs]
    for c, (yc, var, loc) in enumerate(zip(ycs, variances, local)):
        yn = yc * lax.rsqrt(var + GN_EPS) * gw_ref[...] + gb_ref[...]
        o_ref[c * CHUNK:(c + 1) * CHUNK, :] = ((yn + loc[5]) * loc[6]).astype(o_ref.dtype)


def _rwkv_mixer(p_main, p_small, mu, w0, w2, a0, a2, g2, k_k, k_a, r_k, gn_w, gn_b, *, col_r, tl, group):
    b, s, _ = p_main.shape
    d = w0.shape[0]
    width = group * HEAD_DIM
    chunks = tl // CHUNK
    lw_, la_, lg_ = w2.shape[0], a2.shape[0], g2.shape[0]
    lwp, lap, lgp = (_round_up(n, LANES) for n in (lw_, la_, lg_))
    assert width % LANES == 0 and d % width == 0 and col_r % width == 0
    assert s % tl == 0 and tl % CHUNK == 0
    assert lwp % lap == 0 and (lwp + lap) % lgp == 0
    rows8 = tl // SUBLANES
    ncol = d // width

    def seq(wd_, cb):
        return pl.BlockSpec((None, tl, wd_), lambda bi, p, i: (bi, i, cb(p)))

    def prev(wd_, cb):
        return pl.BlockSpec((None, SUBLANES, wd_), lambda bi, p, i: (bi, jnp.maximum(i * rows8 - 1, 0), cb(p)))

    def par(rows=1):
        return pl.BlockSpec((rows, width), lambda bi, p, i: (0, p))

    def const(shape):
        return pl.BlockSpec(shape, lambda bi, p, i: (0,) * len(shape))

    rkv_cols = [lambda p, j=j: col_r // width + j * ncol + p for j in range(3)]
    lora_cols = [lambda p: 0, lambda p: lwp // lap, lambda p: (lwp + lap) // lgp]
    lora_w = [lwp, lap, lgp]
    tril = jnp.tril(jnp.ones((CHUNK, CHUNK), BF16))
    blk = jnp.arange(width) // HEAD_DIM
    bd = (blk[:, None] == blk[None, :]).astype(BF16)
    mu_l = mu[3 * d:]
    w2h, w2l = _split_bf16(_pad_rows(w2, lwp), 2)
    a2h, a2l = _split_bf16(_pad_rows(a2, lap), 2)
    return pl.pallas_call(
        functools.partial(_rwkv_scan_kernel, group=group, chunks=chunks),
        grid=(b, ncol, s // tl),
        in_specs=([seq(width, c) for c in rkv_cols] + [prev(width, c) for c in rkv_cols]
                  + [seq(w, c) for w, c in zip(lora_w, lora_cols)] + [prev(w, c) for w, c in zip(lora_w, lora_cols)]
                  + [par()] * 3 + [const((1, w)) for w in lora_w] + [par()] * 2
                  + [par(lwp)] * 2 + [par(lap)] * 2 + [par(lgp)] + [par()] * 5
                  + [const((CHUNK, CHUNK)), const((width, width))]),
        out_specs=pl.BlockSpec((None, tl, width), lambda bi, p, i: (bi, i, p)),
        out_shape=jax.ShapeDtypeStruct((b, s, d), BF16),
        scratch_shapes=[pltpu.VMEM((HEAD_DIM, width), F32)],
        compiler_params=_params("parallel", "parallel", "arbitrary"),
        name="rwkv_scan",
    )(p_main, p_main, p_main, p_main, p_main, p_main, p_small, p_small, p_small, p_small, p_small, p_small,
      mu[None, :d], mu[None, d:2 * d], mu[None, 2 * d:3 * d],
      _pad_vec(mu_l[:lw_], lwp)[None], _pad_vec(mu_l[lw_:lw_ + la_], lap)[None], _pad_vec(mu_l[lw_ + la_:], lgp)[None],
      w0[None], a0[None], w2h, w2l, a2h, a2l, _pad_rows(g2, lgp).astype(BF16),
      k_k[None], k_a[None], r_k[None], gn_w[None], gn_b[None], tril, bd)


def _pad_cols(w, n):
    return jnp.pad(w, ((0, 0), (0, n - w.shape[1])))


def _pad_rows(w, n):
    return jnp.pad(w, ((0, n - w.shape[0]), (0, 0)))


def _pad_vec(v, n):
    return jnp.pad(v, (0, n - v.shape[0]))


def _round_up(n, m):
    return -(-n // m) * m


def _layer(h, norm1_g, w_in, conv_w, conv_b, dt_bias, a_log, d_skip, ssd_norm_g, mu, w0, w2, a0, a2, g2,
           k_k, k_a, r_k, gn_w, gn_b, w_out, norm2_g, w_gate, w_up, w_down, out_g):
    b, s, d = h.shape
    m = b * s
    d_ssd = ssd_norm_g.shape[0]
    d_rwkv = w0.shape[0]
    d_bc = 2 * SSD_GROUPS * SSD_STATE
    heads = d_ssd // HEAD_DIM
    lw_, la_, lg_ = w2.shape[0], a2.shape[0], g2.shape[0]
    lwp, lap, lgp = (_round_up(n, LANES) for n in (lw_, la_, lg_))

    o_xbc = d_ssd
    o_dt = o_xbc + d_ssd + d_bc
    o_r = o_dt + heads
    o_wd = o_r + 3 * d_rwkv
    o_ad = o_wd + lw_
    o_gd = o_ad + la_
    w_in = w_in.astype(BF16)
    w_rkv = w_in[:, o_r:o_wd]
    w_small = jnp.concatenate([_pad_cols(w_in[:, o_wd:o_ad], lwp), _pad_cols(w_in[:, o_ad:o_gd], lap),
                               _pad_cols(w_in[:, o_gd:], lgp), _pad_cols(w_in[:, o_dt:o_r], LANES)], axis=1)
    col_dt = lwp + lap + lgp

    tm = min(1024, m)
    u = _rmsnorm(h.reshape(m, d), norm1_g, tm=min(512, m))
    p_ssd = _matmul(u, w_in, n=o_dt, tm=tm, tn=1024, name="in_proj_ssd").reshape(b, s, -1)
    p_rkv = _matmul(u, w_rkv, tm=tm, tn=1024, name="in_proj_rkv").reshape(b, s, -1)
    p_small = _matmul(u, w_small, tm=tm, tn=w_small.shape[1], name="in_proj_small").reshape(b, s, -1)

    y_ssd = _ssd_mixer(p_ssd, p_small, conv_w, conv_b, dt_bias, a_log, d_skip, ssd_norm_g,
                       d_ssd=d_ssd, col_xs=o_xbc, col_bc=o_xbc + d_ssd, col_dt=col_dt, tl=min(8 * CHUNK, s))

    scan_tl = min(1024, s)
    y_rwkv = _rwkv_mixer(p_rkv, p_small, mu, w0, w2, a0, a2, g2, k_k, k_a, r_k, gn_w, gn_b,
                         col_r=0, tl=scan_tl, group=4)

    tm2 = min(512, m)
    h1, v2 = _row_matmul([y_ssd.reshape(m, d_ssd), y_rwkv.reshape(m, d_rwkv)], w_out.astype(BF16), h.reshape(m, d),
                         norm2_g, tm=tm2, tn=512, final=False, name="out_proj")
    ff = _gate_up(v2, w_gate.astype(BF16), w_up.astype(BF16), tm=tm, tn=512)
    out = _row_matmul([ff], w_down.astype(BF16), h1, out_g, tm=tm2, tn=512, final=True, name="down_proj")
    return out.reshape(b, s, d)


def kernel(x, norm1_g, w_in, ssd_conv_w, ssd_conv_b, ssd_dt_bias, ssd_A_log, ssd_D, ssd_norm_g, rwkv_mu, rwkv_w0,
           rwkv_w2, rwkv_a0, rwkv_a2, rwkv_g2, rwkv_k_k, rwkv_k_a, rwkv_r_k, rwkv_gn_w, rwkv_gn_b, w_out,
           norm2_g, w_gate, w_up, w_down, norm_f_g):
    assert w_in.shape[0] == 1, "single-layer block"
    return _layer(x, norm1_g[0], w_in[0], ssd_conv_w[0], ssd_conv_b[0], ssd_dt_bias[0], ssd_A_log[0], ssd_D[0],
                  ssd_norm_g[0], rwkv_mu[0], rwkv_w0[0], rwkv_w2[0], rwkv_a0[0], rwkv_a2[0], rwkv_g2[0],
                  rwkv_k_k[0], rwkv_k_a[0], rwkv_r_k[0], rwkv_gn_w[0], rwkv_gn_b[0], w_out[0], norm2_g[0],
                  w_gate[0], w_up[0], w_down[0], norm_f_g)
```

```python
import functools

import jax
import jax.numpy as jnp
from jax import lax
from jax.experimental import pallas as pl
from jax.experimental.pallas import tpu as pltpu

F32 = jnp.float32
BF16 = jnp.bfloat16
HIGHEST = lax.Precision.HIGHEST

HEAD_DIM = 64
CHUNK = 64
SSD_GROUPS = 4
SSD_STATE = 128
CONV_WIDTH = 4
RMS_EPS = 1e-6
GATED_NORM_EPS = 1e-5
GN_EPS = 64e-5

LANES = 128
SUBLANES = 8
BF16_ROWS = 2 * SUBLANES
VMEM_LIMIT_BYTES = 56 * 1024 * 1024

NT_DIMS = (((1,), (1,)), ((), ()))
TN_DIMS = (((0,), (0,)), ((), ()))


def _sigmoid(x):
    return jax.nn.sigmoid(x)


def _softplus(x):
    return jnp.maximum(x, 0.0) + jnp.log1p(jnp.exp(-jnp.abs(x)))


def _rms(h, eps):
    return h * lax.rsqrt(jnp.mean(h * h, axis=-1, keepdims=True) + eps)


def _params(*sem):
    return pltpu.CompilerParams(dimension_semantics=sem, vmem_limit_bytes=VMEM_LIMIT_BYTES)


def _mm(a, b):
    return jnp.dot(a, b, preferred_element_type=F32)


def _norm_matmul_kernel(x_ref, g_ref, w_ref, o_ref, u_ref):
    @pl.when(pl.program_id(1) == 0)
    def _():
        u_ref[...] = (_rms(x_ref[...], RMS_EPS) * g_ref[...]).astype(BF16)

    o_ref[...] = _mm(u_ref[...], w_ref[...])


def _norm_matmul(x, g, w, *, tm, tn, name, n):
    m, kd = x.shape
    assert n % tn == 0 and n <= w.shape[1]
    row = pl.BlockSpec((tm, kd), lambda i, j: (i, 0))
    return pl.pallas_call(
        _norm_matmul_kernel,
        grid=(m // tm, n // tn),
        in_specs=[row, pl.BlockSpec((1, kd), lambda i, j: (0, 0)), pl.BlockSpec((kd, tn), lambda i, j: (0, j))],
        out_specs=[pl.BlockSpec((tm, tn), lambda i, j: (i, j)), row],
        out_shape=[jax.ShapeDtypeStruct((m, n), F32), jax.ShapeDtypeStruct((m, kd), BF16)],
        compiler_params=_params("parallel", "arbitrary"),
        name=name,
    )(x, g.reshape(1, kd), w)


def _matmul_kernel(a_ref, w_ref, o_ref):
    o_ref[...] = _mm(a_ref[...], w_ref[...]).astype(o_ref.dtype)


def _matmul(a, w, *, tm, tn, name, n=None):
    m, kd = a.shape
    n = w.shape[1] if n is None else n
    assert n % tn == 0 and n <= w.shape[1]
    return pl.pallas_call(
        _matmul_kernel,
        grid=(m // tm, n // tn),
        in_specs=[pl.BlockSpec((tm, kd), lambda i, j: (i, 0)), pl.BlockSpec((kd, tn), lambda i, j: (0, j))],
        out_specs=pl.BlockSpec((tm, tn), lambda i, j: (i, j)),
        out_shape=jax.ShapeDtypeStruct((m, n), F32),
        compiler_params=_params("parallel", "parallel"),
        name=name,
    )(a, w)


def _res_matmul_kernel(a1_ref, a2_ref, w1_ref, w2_ref, res_ref, o_ref):
    o_ref[...] = _mm(a1_ref[...], w1_ref[...]) + _mm(a2_ref[...], w2_ref[...]) + res_ref[...]


def _res_matmul(a1, a2, w, res, *, tm, tn, name):
    m, n = res.shape
    kd = a1.shape[1]
    assert a2.shape[1] == kd and w.shape[0] == 2 * kd and n % tn == 0
    a_spec = pl.BlockSpec((tm, kd), lambda i, j: (i, 0))
    tile = pl.BlockSpec((tm, tn), lambda i, j: (i, j))
    return pl.pallas_call(
        _res_matmul_kernel,
        grid=(m // tm, n // tn),
        in_specs=[a_spec, a_spec, pl.BlockSpec((kd, tn), lambda i, j: (0, j)),
                  pl.BlockSpec((kd, tn), lambda i, j: (1, j)), tile],
        out_specs=tile,
        out_shape=jax.ShapeDtypeStruct((m, n), F32),
        compiler_params=_params("parallel", "parallel"),
        name=name,
    )(a1, a2, w, w, res)


def _final_matmul_kernel(a_ref, w_ref, res_ref, g_ref, h_ref, *, nj, tn):
    j = pl.program_id(1)
    part = _mm(a_ref[...], w_ref[...])
    for jj in range(nj):
        @pl.when(j == jj)
        def _(jj=jj):
            cols = slice(jj * tn, (jj + 1) * tn)
            h_ref[:, cols] = part + res_ref[:, cols]

    @pl.when(j == nj - 1)
    def _():
        h_ref[...] = _rms(h_ref[...], RMS_EPS) * g_ref[...]


def _final_matmul(a, w, res, norm_g, *, tm, tn, name):
    m, n = res.shape
    kd = a.shape[1]
    assert kd == w.shape[0] and n % tn == 0
    nj = n // tn
    row = pl.BlockSpec((tm, n), lambda i, j: (i, 0))
    return pl.pallas_call(
        functools.partial(_final_matmul_kernel, nj=nj, tn=tn),
        grid=(m // tm, nj),
        in_specs=[pl.BlockSpec((tm, kd), lambda i, j: (i, 0)), pl.BlockSpec((kd, tn), lambda i, j: (0, j)),
                  row, pl.BlockSpec((1, n), lambda i, j: (0, 0))],
        out_specs=row,
        out_shape=jax.ShapeDtypeStruct((m, n), F32),
        compiler_params=_params("parallel", "arbitrary"),
        name=name,
    )(a, w, res, norm_g.reshape(1, n))


def _norm_gate_up_kernel(x_ref, g_ref, wg_ref, wu_ref, o_ref, u_scr):
    @pl.when(pl.program_id(1) == 0)
    def _():
        u_scr[...] = (_rms(x_ref[...], RMS_EPS) * g_ref[...]).astype(BF16)

    a = u_scr[...]
    gate = _mm(a, wg_ref[...])
    up = _mm(a, wu_ref[...])
    o_ref[...] = (gate * _sigmoid(gate) * up).astype(o_ref.dtype)


def _norm_gate_up(x, g, wg, wu, *, tm, tn):
    m, kd = x.shape
    n = wg.shape[1]
    return pl.pallas_call(
        _norm_gate_up_kernel,
        grid=(m // tm, n // tn),
        in_specs=[pl.BlockSpec((tm, kd), lambda i, j: (i, 0)), pl.BlockSpec((1, kd), lambda i, j: (0, 0)),
                  pl.BlockSpec((kd, tn), lambda i, j: (0, j)),
                  pl.BlockSpec((kd, tn), lambda i, j: (0, j))],
        out_specs=pl.BlockSpec((tm, tn), lambda i, j: (i, j)),
        out_shape=jax.ShapeDtypeStruct((m, n), BF16),
        scratch_shapes=[pltpu.VMEM((tm, kd), BF16)],
        compiler_params=_params("parallel", "arbitrary"),
        name="gate_up",
    )(x, g.reshape(1, kd), wg, wu)


def _shift_rows(cur, prev8, d, first):
    rolled = pltpu.roll(cur, d, axis=0)
    prolled = pltpu.roll(prev8, d, axis=0)
    if first is not None:
        prolled = jnp.where(first, 0.0, prolled)
    row = lax.broadcasted_iota(jnp.int32, prolled.shape, 0)
    top = jnp.where(row < d, prolled, rolled[0:SUBLANES])
    return jnp.concatenate([top, rolled[SUBLANES:]], axis=0)


def _ssd_kernel(z_ref, xs_ref, bc_ref, xsp_ref, bcp_ref, dt_ref, cwx_ref, cwbc_ref, cbx_ref, cbbc_ref,
                dtb_ref, alog_ref, dexp_ref, ng_ref, e_ref, tril_ref, o_ref, h_ref, y_ref, *, heads, chunks):
    first = pl.program_id(1) == 0

    @pl.when(first)
    def _():
        h_ref[...] = jnp.zeros_like(h_ref)

    for c in range(chunks):
        _ssd_chunk(c, first, z_ref, xs_ref, bc_ref, xsp_ref, bcp_ref, dt_ref, cwx_ref, cwbc_ref, cbx_ref, cbbc_ref,
                   dtb_ref, alog_ref, dexp_ref, ng_ref, e_ref, tril_ref, o_ref, h_ref, y_ref, heads)


def _ssd_chunk(c, first, z_ref, xs_ref, bc_ref, xsp_ref, bcp_ref, dt_ref, cwx_ref, cwbc_ref, cbx_ref, cbbc_ref,
               dtb_ref, alog_ref, dexp_ref, ng_ref, e_ref, tril_ref, o_ref, h_ref, y_ref, heads):
    hpg = heads // SSD_GROUPS
    gw = hpg * HEAD_DIM
    rows = slice(c * CHUNK, (c + 1) * CHUNK)

    def conv_silu(cur_ref, prev_ref, w_ref, b_ref):
        cur = cur_ref[rows, :]
        if c == 0:
            prev8, zero_if = prev_ref[...], first
        else:
            prev8, zero_if = cur_ref[c * CHUNK - SUBLANES:c * CHUNK, :], None
        acc = cur * w_ref[CONV_WIDTH - 1:CONV_WIDTH, :] + b_ref[...]
        for d in range(1, CONV_WIDTH):
            acc = acc + _shift_rows(cur, prev8, d, zero_if) * w_ref[CONV_WIDTH - 1 - d:CONV_WIDTH - d, :]
        return acc * _sigmoid(acc)

    xs = conv_silu(xs_ref, xsp_ref, cwx_ref, cbx_ref)
    bc = conv_silu(bc_ref, bcp_ref, cwbc_ref, cbbc_ref)
    xs_bf = xs.astype(BF16)

    dt = _softplus(dt_ref[rows, :] + dtb_ref[...])
    adt = dt * (-jnp.exp(alog_ref[...]))
    cs = jnp.dot(tril_ref[...], adt, precision=HIGHEST, preferred_element_type=F32)
    cs_last = cs[CHUNK - 1:CHUNK, :]
    decay_states = jnp.exp(cs_last - cs)
    chunk_decay = jnp.exp(cs_last)
    exp_in = jnp.concatenate([dt * decay_states, jnp.exp(cs), jnp.broadcast_to(chunk_decay, (BF16_ROWS, LANES))],
                             axis=0)
    exp_hi = exp_in.astype(BF16)
    exp_lo = (exp_in - exp_hi.astype(F32)).astype(BF16)
    expanded = _mm(jnp.concatenate([exp_hi, exp_lo], axis=0), e_ref[...])
    expanded = expanded[0:exp_in.shape[0]] + expanded[exp_in.shape[0]:]
    xscale = expanded[0:CHUNK]
    ecs = expanded[CHUNK:2 * CHUNK]
    cdecay = expanded[2 * CHUNK:2 * CHUNK + 1]

    cs_t = cs.T
    dt_t = dt.T
    li = lax.broadcasted_iota(jnp.int32, (CHUNK, CHUNK), 0)
    si = lax.broadcasted_iota(jnp.int32, (CHUNK, CHUNK), 1)
    causal = li >= si
    lane = lax.broadcasted_iota(jnp.int32, (CHUNK, LANES), 1)

    cbs = []
    for g in range(SSD_GROUPS):
        bm = bc[:, g * SSD_STATE:(g + 1) * SSD_STATE].astype(BF16)
        cm = bc[:, (SSD_GROUPS + g) * SSD_STATE:(SSD_GROUPS + g + 1) * SSD_STATE].astype(BF16)
        cbs.append(lax.dot_general(cm, bm, NT_DIMS, preferred_element_type=F32))
        cols = slice(g * gw, (g + 1) * gw)
        xsc = (xs[:, cols] * xscale[:, cols]).astype(BF16)
        states = lax.dot_general(bm, xsc, TN_DIMS, preferred_element_type=F32)
        h = h_ref[g]
        y_ref[rows, cols] = _mm(cm, h.astype(BF16)) * ecs[:, cols]
        h_ref[g] = h * cdecay[:, cols] + states

    for q in range(heads // 2):
        mats = []
        for e in (2 * q, 2 * q + 1):
            diff = cs[:, e:e + 1] - cs_t[e:e + 1, :]
            lmat = jnp.exp(jnp.where(causal, diff, -jnp.inf))
            mats.append(cbs[e // hpg] * lmat * dt_t[e:e + 1, :])
        lhs = jnp.concatenate(mats, axis=0).astype(BF16)
        cols = slice(q * LANES, (q + 1) * LANES)
        res = _mm(lhs, xs_bf[:, cols])
        y_ref[rows, cols] += jnp.where(lane < HEAD_DIM, res[0:CHUNK], res[CHUNK:2 * CHUNK])

    z = z_ref[rows, :]
    y = (y_ref[rows, :] + dexp_ref[...] * xs) * (z * _sigmoid(z))
    for g in range(SSD_GROUPS):
        cols = slice(g * gw, (g + 1) * gw)
        o_ref[rows, cols] = (_rms(y[:, cols], GATED_NORM_EPS) * ng_ref[:, cols]).astype(o_ref.dtype)


def _ssd_mixer(p_main, p_small, conv_w, conv_b, dt_bias, a_log, d_skip, norm_g, *, d_ssd, col_xs, col_bc, col_dt,
               tl):
    b, s, _ = p_main.shape
    heads = d_ssd // HEAD_DIM
    d_bc = 2 * SSD_GROUPS * SSD_STATE
    assert s % tl == 0 and tl % CHUNK == 0 and heads % (2 * SSD_GROUPS) == 0 and heads <= LANES
    assert col_xs % d_ssd == 0 and col_bc % d_bc == 0 and col_dt % LANES == 0
    rows8 = tl // SUBLANES

    def prev_map(cb):
        return lambda bi, i: (bi, jnp.maximum(i * rows8 - 1, 0), cb)

    def const(shape):
        return pl.BlockSpec(shape, lambda bi, i: (0,) * len(shape))

    pad = LANES - heads
    e_mat = (jnp.arange(LANES)[:, None] == (jnp.arange(d_ssd)[None, :] // HEAD_DIM)).astype(BF16)
    tril = jnp.tril(jnp.ones((CHUNK, CHUNK), F32))
    return pl.pallas_call(
        functools.partial(_ssd_kernel, heads=heads, chunks=tl // CHUNK),
        grid=(b, s // tl),
        in_specs=[
            pl.BlockSpec((None, tl, d_ssd), lambda bi, i: (bi, i, 0)),
            pl.BlockSpec((None, tl, d_ssd), lambda bi, i: (bi, i, col_xs // d_ssd)),
            pl.BlockSpec((None, tl, d_bc), lambda bi, i: (bi, i, col_bc // d_bc)),
            pl.BlockSpec((None, SUBLANES, d_ssd), prev_map(col_xs // d_ssd)),
            pl.BlockSpec((None, SUBLANES, d_bc), prev_map(col_bc // d_bc)),
            pl.BlockSpec((None, tl, LANES), lambda bi, i: (bi, i, col_dt // LANES)),
            const((CONV_WIDTH, d_ssd)), const((CONV_WIDTH, d_bc)), const((1, d_ssd)), const((1, d_bc)),
            const((1, LANES)), const((1, LANES)), const((1, d_ssd)), const((1, d_ssd)),
            const((LANES, d_ssd)), const((CHUNK, CHUNK)),
        ],
        out_specs=pl.BlockSpec((None, tl, d_ssd), lambda bi, i: (bi, i, 0)),
        out_shape=jax.ShapeDtypeStruct((b, s, d_ssd), BF16),
        scratch_shapes=[pltpu.VMEM((SSD_GROUPS, SSD_STATE, d_ssd // SSD_GROUPS), F32),
                        pltpu.VMEM((tl, d_ssd), F32)],
        compiler_params=_params("parallel", "arbitrary"),
        name="ssd_mixer",
    )(p_main, p_main, p_main, p_main, p_main, p_small,
      conv_w[:, :d_ssd], conv_w[:, d_ssd:], conv_b[None, :d_ssd], conv_b[None, d_ssd:],
      jnp.pad(dt_bias, (0, pad))[None], jnp.pad(a_log, (0, pad))[None],
      jnp.repeat(d_skip, HEAD_DIM)[None], norm_g[None], e_mat, tril)


def _split_bf16(x, terms):
    parts = []
    for _ in range(terms):
        p = x.astype(BF16)
        parts.append(p)
        x = x - p.astype(F32)
    return parts


def _mm_split(x, w_hi, w_lo):
    x_hi, x_lo = _split_bf16(x, 2)
    n = x.shape[0]
    first = _mm(jnp.concatenate([x_hi, x_lo], axis=0), w_hi)
    return first[0:n] + first[n:] + _mm(x_hi, w_lo)


def _lockstep(gens):
    results = [None] * len(gens)
    live = list(range(len(gens)))
    while live:
        for i in list(live):
            try:
                next(gens[i])
            except StopIteration as done:
                results[i] = done.value
                live.remove(i)
    return results


def _rwkv_scan_kernel(rin_ref, kin_ref, vin_ref, rp_ref, kp_ref, vp_ref, wd_ref, ad_ref, gd_ref,
                      wdp_ref, adp_ref, gdp_ref, mur_ref, muk_ref, muv_ref, muwd_ref, muad_ref, mugd_ref,
                      w0_ref, a0_ref, w2h_ref, w2l_ref, a2h_ref, a2l_ref, g2_ref,
                      kk_ref, ka_ref, rk_ref, gw_ref, gb_ref, tril_ref, bd_ref, o_ref,
                      st_ref, *, group, chunks):
    width = group * HEAD_DIM
    first = pl.program_id(2) == 0

    @pl.when(first)
    def _():
        st_ref[...] = jnp.zeros_like(st_ref)

    def token_shift(cur_ref, prev_ref, mu_ref, c):
        cur = cur_ref[c * CHUNK:(c + 1) * CHUNK, :]
        if c == 0:
            prev = _shift_rows(cur, prev_ref[...], 1, first)
        else:
            prev = _shift_rows(cur, cur_ref[c * CHUNK - SUBLANES:c * CHUNK, :], 1, None)
        return cur + (prev - cur) * mu_ref[...]

    row = lax.broadcasted_iota(jnp.int32, (CHUNK, width), 0)
    lane = lax.broadcasted_iota(jnp.int32, (CHUNK, width), 1)
    col = lane % HEAD_DIM
    lane_head = lane // HEAD_DIM
    strict = col < row
    incl = col <= row
    eye = (col == row).astype(F32)
    bd_b = bd_ref[...]
    bd_rows = lax.broadcasted_iota(jnp.int32, (width, width), 0) // HEAD_DIM
    bd_cols = lax.broadcasted_iota(jnp.int32, (width, width), 1) // HEAD_DIM
    bd_mask = bd_rows == bd_cols

    def level_mask(m):
        return ((row // (2 * m)) == (col // (2 * m))) & ((row // m) != (col // m))

    heads_per_tile = LANES // HEAD_DIM
    half = lax.broadcasted_iota(jnp.int32, (1, LANES), 1) // HEAD_DIM
    half_masks = [(half == j).astype(BF16) for j in range(heads_per_tile)]
    zero_tile = jnp.zeros((CHUNK, LANES), BF16)

    def stack(x):
        blocks = []
        for h in range(group):
            t = h // heads_per_tile
            kept = x[:, t * LANES:(t + 1) * LANES] * half_masks[h % heads_per_tile]
            blocks.append(jnp.concatenate([kept if tt == t else zero_tile for tt in range(width // LANES)], axis=1))
        return jnp.concatenate(blocks, axis=0)

    def head_sum(x):
        n = x.shape[0]
        res = _mm(jnp.concatenate(_split_bf16(x, 2), axis=0), bd_b)
        return res[0:n] + res[n:]

    def cumsum_rows(x):
        res = _mm(tril_ref[...], jnp.concatenate(_split_bf16(x, 3), axis=1))
        return res[:, 0:width] + res[:, width:2 * width] + res[:, 2 * width:]

    def local_terms(c):
        r = token_shift(rin_ref, rp_ref, mur_ref, c)
        k = token_shift(kin_ref, kp_ref, muk_ref, c)
        v = token_shift(vin_ref, vp_ref, muv_ref, c)
        dw = _mm_split(jnp.tanh(token_shift(wd_ref, wdp_ref, muwd_ref, c)), w2h_ref[...], w2l_ref[...])
        lw = -jnp.exp(-_softplus(-(w0_ref[...] + dw)) - 0.5)
        yield
        da = _mm_split(token_shift(ad_ref, adp_ref, muad_ref, c), a2h_ref[...], a2l_ref[...])
        a = _sigmoid(a0_ref[...] + da)
        gate = _mm(_sigmoid(token_shift(gd_ref, gdp_ref, mugd_ref, c)).astype(BF16), g2_ref[...])
        yield
        kk = k * kk_ref[...]
        ss = head_sum(kk * kk)
        yield
        kk = kk / jnp.maximum(jnp.sqrt(ss), 1e-12)
        k2 = k * (1.0 + (a - 1.0) * ka_ref[...])
        bonus = head_sum(r * k2 * rk_ref[...]) * v
        yield
        cl = cumsum_rows(lw)
        yield
        e_pos = jnp.exp(cl)
        e_neg = jnp.exp(-cl)
        g_last = e_pos[CHUNK - 1:CHUNK, :]
        rt = r * e_pos
        at_b = (-kk * jnp.exp(cl - lw)).astype(BF16)
        rt_b = rt.astype(BF16)
        bt_b = (kk * a * e_neg).astype(BF16)
        kt_b = (k2 * e_neg).astype(BF16)
        v_b = v.astype(BF16)

        ar = jnp.concatenate([at_b, rt_b], axis=0)
        att_b = lax.dot_general(ar, stack(bt_b), NT_DIMS, preferred_element_type=F32)
        att_k = lax.dot_general(ar, stack(kt_b), NT_DIMS, preferred_element_type=F32)
        yield
        n_ab = jnp.where(strict, att_b[0:CHUNK], 0.0)
        a_ak = jnp.where(strict, att_k[0:CHUNK], 0.0)
        a_rb = jnp.where(incl, att_b[CHUNK:], 0.0).astype(BF16)
        a_rk = jnp.where(incl, att_k[CHUNK:], 0.0)

        tinv = eye + jnp.where(level_mask(1), n_ab, 0.0)
        m = 2
        while m < CHUNK:
            off = jnp.where(level_mask(m), n_ab, 0.0).astype(BF16)
            tb = tinv.astype(BF16)
            mid = _mm(tb, stack(off))
            yield
            tinv = tinv + _mm(mid.astype(BF16), stack(tb))
            yield
            m *= 2
        tb = tinv.astype(BF16)

        xy = _mm(jnp.concatenate([a_ak, a_rk], axis=0).astype(BF16), stack(v_b))
        p_b = _mm(tb, stack(at_b)).astype(BF16)
        yield
        u0_b = _mm(tb, stack(xy[0:CHUNK].astype(BF16))).astype(BF16)
        q_b = (rt + _mm(a_rb, stack(p_b))).astype(BF16)
        mt = lax.dot_general(p_b, bt_b, TN_DIMS, preferred_element_type=F32)
        mt_b = (jnp.where(bd_mask, mt, 0.0) * g_last).astype(BF16)
        yield
        y0 = xy[CHUNK:] + _mm(a_rb, stack(u0_b))
        nfull = lax.dot_general(jnp.concatenate([u0_b, v_b], axis=0), jnp.concatenate([bt_b, kt_b], axis=0),
                                TN_DIMS, preferred_element_type=F32)
        nt = jnp.where(lane_head == 0, nfull[0:HEAD_DIM], 0.0)
        for h in range(1, group):
            nt = nt + jnp.where(lane_head == h, nfull[h * HEAD_DIM:(h + 1) * HEAD_DIM], 0.0)
        return q_b, y0, mt_b, nt * g_last, g_last, bonus, gate

    local = _lockstep([local_terms(c) for c in range(chunks)])
    st = st_ref[...]
    ys = []
    for q_b, y0, mt_b, nt, g_last, _, _ in local:
        st_b = st.astype(BF16)
        ys.append(lax.dot_general(q_b, stack(st_b), NT_DIMS, preferred_element_type=F32) + y0)
        st = st * g_last + _mm(st_b, mt_b) + nt
    st_ref[...] = st
    ycs = [y - head_sum(y) * (1.0 / HEAD_DIM) for y in ys]
    variances = [head_sum(yc * yc) * (1.0 / HEAD_DIM) for yc in ycs]
    for c, (yc, var, loc) in enumerate(zip(ycs, variances, local)):
        yn = yc * lax.rsqrt(var + GN_EPS) * gw_ref[...] + gb_ref[...]
        o_ref[c * CHUNK:(c + 1) * CHUNK, :] = ((yn + loc[5]) * loc[6]).astype(o_ref.dtype)


def _rwkv_mixer(p_main, p_small, mu, w0, w2, a0, a2, g2, k_k, k_a, r_k, gn_w, gn_b, *, col_r, tl, group):
    b, s, _ = p_main.shape
    d = w0.shape[0]
    width = group * HEAD_DIM
    chunks = tl // CHUNK
    lw_, la_, lg_ = w2.shape[0], a2.shape[0], g2.shape[0]
    lwp, lap, lgp = (_round_up(n, LANES) for n in (lw_, la_, lg_))
    assert width % LANES == 0 and d % width == 0 and col_r % width == 0
    assert s % tl == 0 and tl % CHUNK == 0
    assert lwp % lap == 0 and (lwp + lap) % lgp == 0
    rows8 = tl // SUBLANES
    ncol = d // width

    def seq(wd_, cb):
        return pl.BlockSpec((None, tl, wd_), lambda bi, p, i: (bi, i, cb(p)))

    def prev(wd_, cb):
        return pl.BlockSpec((None, SUBLANES, wd_), lambda bi, p, i: (bi, jnp.maximum(i * rows8 - 1, 0), cb(p)))

    def par(rows=1):
        return pl.BlockSpec((rows, width), lambda bi, p, i: (0, p))

    def const(shape):
        return pl.BlockSpec(shape, lambda bi, p, i: (0,) * len(shape))

    rkv_cols = [lambda p, j=j: col_r // width + j * ncol + p for j in range(3)]
    lora_cols = [lambda p: 0, lambda p: lwp // lap, lambda p: (lwp + lap) // lgp]
    lora_w = [lwp, lap, lgp]
    tril = jnp.tril(jnp.ones((CHUNK, CHUNK), BF16))
    blk = jnp.arange(width) // HEAD_DIM
    bd = (blk[:, None] == blk[None, :]).astype(BF16)
    mu_l = mu[3 * d:]
    w2h, w2l = _split_bf16(_pad_rows(w2, lwp), 2)
    a2h, a2l = _split_bf16(_pad_rows(a2, lap), 2)
    return pl.pallas_call(
        functools.partial(_rwkv_scan_kernel, group=group, chunks=chunks),
        grid=(b, ncol, s // tl),
        in_specs=([seq(width, c) for c in rkv_cols] + [prev(width, c) for c in rkv_cols]
                  + [seq(w, c) for w, c in zip(lora_w, lora_cols)] + [prev(w, c) for w, c in zip(lora_w, lora_cols)]
                  + [par()] * 3 + [const((1, w)) for w in lora_w] + [par()] * 2
                  + [par(lwp)] * 2 + [par(lap)] * 2 + [par(lgp)] + [par()] * 5
                  + [const((CHUNK, CHUNK)), const((width, width))]),
        out_specs=pl.BlockSpec((None, tl, width), lambda bi, p, i: (bi, i, p)),
        out_shape=jax.ShapeDtypeStruct((b, s, d), BF16),
        scratch_shapes=[pltpu.VMEM((HEAD_DIM, width), F32)],
        compiler_params=_params("parallel", "parallel", "arbitrary"),
        name="rwkv_scan",
    )(p_main, p_main, p_main, p_main, p_main, p_main, p_small, p_small, p_small, p_small, p_small, p_small,
      mu[None, :d], mu[None, d:2 * d], mu[None, 2 * d:3 * d],
      _pad_vec(mu_l[:lw_], lwp)[None], _pad_vec(mu_l[lw_:lw_ + la_], lap)[None], _pad_vec(mu_l[lw_ + la_:], lgp)[None],
      w0[None], a0[None], w2h, w2l, a2h, a2l, _pad_rows(g2, lgp).astype(BF16),
      k_k[None], k_a[None], r_k[None], gn_w[None], gn_b[None], tril, bd)


def _pad_cols(w, n):
    return jnp.pad(w, ((0, 0), (0, n - w.shape[1])))


def _pad_rows(w, n):
    return jnp.pad(w, ((0, n - w.shape[0]), (0, 0)))


def _pad_vec(v, n):
    return jnp.pad(v, (0, n - v.shape[0]))


def _round_up(n, m):
    return -(-n // m) * m


def _layer(h, norm1_g, w_in, conv_w, conv_b, dt_bias, a_log, d_skip, ssd_norm_g, mu, w0, w2, a0, a2, g2,
           k_k, k_a, r_k, gn_w, gn_b, w_out, norm2_g, w_gate, w_up, w_down, out_g):
    b, s, d = h.shape
    m = b * s
    d_ssd = ssd_norm_g.shape[0]
    d_rwkv = w0.shape[0]
    d_bc = 2 * SSD_GROUPS * SSD_STATE
    heads = d_ssd // HEAD_DIM
    lw_, la_, lg_ = w2.shape[0], a2.shape[0], g2.shape[0]
    lwp, lap, lgp = (_round_up(n, LANES) for n in (lw_, la_, lg_))

    o_xbc = d_ssd
    o_dt = o_xbc + d_ssd + d_bc
    o_r = o_dt + heads
    o_wd = o_r + 3 * d_rwkv
    o_ad = o_wd + lw_
    o_gd = o_ad + la_
    w_in = w_in.astype(BF16)
    w_rkv = w_in[:, o_r:o_wd]
    w_small = jnp.concatenate([_pad_cols(w_in[:, o_wd:o_ad], lwp), _pad_cols(w_in[:, o_ad:o_gd], lap),
                               _pad_cols(w_in[:, o_gd:], lgp), _pad_cols(w_in[:, o_dt:o_r], LANES)], axis=1)
    col_dt = lwp + lap + lgp

    tm = min(1024, m)
    p_ssd, u = _norm_matmul(h.reshape(m, d), norm1_g, w_in, n=o_dt, tm=tm, tn=1024, name="in_proj_ssd")
    p_ssd = p_ssd.reshape(b, s, -1)
    p_rkv = _matmul(u, w_rkv, tm=tm, tn=1024, name="in_proj_rkv").reshape(b, s, -1)
    p_small = _matmul(u, w_small, tm=tm, tn=w_small.shape[1], name="in_proj_small").reshape(b, s, -1)

    y_ssd = _ssd_mixer(p_ssd, p_small, conv_w, conv_b, dt_bias, a_log, d_skip, ssd_norm_g,
                       d_ssd=d_ssd, col_xs=o_xbc, col_bc=o_xbc + d_ssd, col_dt=col_dt, tl=min(8 * CHUNK, s))

    scan_tl = min(1024, s)
    y_rwkv = _rwkv_mixer(p_rkv, p_small, mu, w0, w2, a0, a2, g2, k_k, k_a, r_k, gn_w, gn_b,
                         col_r=0, tl=scan_tl, group=4)

    h1 = _res_matmul(y_ssd.reshape(m, d_ssd), y_rwkv.reshape(m, d_rwkv), w_out.astype(BF16), h.reshape(m, d),
                     tm=tm, tn=512, name="out_proj")
    ff = _norm_gate_up(h1, norm2_g, w_gate.astype(BF16), w_up.astype(BF16), tm=tm, tn=512)
    out = _final_matmul(ff, w_down.astype(BF16), h1, out_g, tm=min(512, m), tn=512, name="down_proj")
    return out.reshape(b, s, d)


def kernel(x, norm1_g, w_in, ssd_conv_w, ssd_conv_b, ssd_dt_bias, ssd_A_log, ssd_D, ssd_norm_g, rwkv_mu, rwkv_w0,
           rwkv_w2, rwkv_a0, rwkv_a2, rwkv_g2, rwkv_k_k, rwkv_k_a, rwkv_r_k, rwkv_gn_w, rwkv_gn_b, w_out,
           norm2_g, w_gate, w_up, w_down, norm_f_g):
    assert w_in.shape[0] == 1, "single-layer block"
    return _layer(x, norm1_g[0], w_in[0], ssd_conv_w[0], ssd_conv_b[0], ssd_dt_bias[0], ssd_A_log[0], ssd_D[0],
                  ssd_norm_g[0], rwkv_mu[0], rwkv_w0[0], rwkv_w2[0], rwkv_a0[0], rwkv_a2[0], rwkv_g2[0],
                  rwkv_k_k[0], rwkv_k_a[0], rwkv_r_k[0], rwkv_gn_w[0], rwkv_gn_b[0], w_out[0], norm2_g[0],
                  w_gate[0], w_up[0], w_down[0], norm_f_g)
```

```python
import functools

import jax
import jax.numpy as jnp
from jax import lax
from jax.experimental import pallas as pl
from jax.experimental.pallas import tpu as pltpu

F32 = jnp.float32
BF16 = jnp.bfloat16
HIGHEST = lax.Precision.HIGHEST

HEAD_DIM = 64
CHUNK = 64
SSD_GROUPS = 4
SSD_STATE = 128
CONV_WIDTH = 4
RMS_EPS = 1e-6
GATED_NORM_EPS = 1e-5
GN_EPS = 64e-5

LANES = 128
SUBLANES = 8
BF16_ROWS = 2 * SUBLANES
VMEM_LIMIT_BYTES = 56 * 1024 * 1024

NT_DIMS = (((1,), (1,)), ((), ()))
TN_DIMS = (((0,), (0,)), ((), ()))


def _sigmoid(x):
    return jax.nn.sigmoid(x)


def _softplus(x):
    return jnp.maximum(x, 0.0) + jnp.log1p(jnp.exp(-jnp.abs(x)))


def _rms(h, eps):
    return h * lax.rsqrt(jnp.mean(h * h, axis=-1, keepdims=True) + eps)


def _params(*sem):
    return pltpu.CompilerParams(dimension_semantics=sem, vmem_limit_bytes=VMEM_LIMIT_BYTES)


def _mm(a, b):
    return jnp.dot(a, b, preferred_element_type=F32)


def _rmsnorm_kernel(x_ref, g_ref, o_ref):
    o_ref[...] = (_rms(x_ref[...], RMS_EPS) * g_ref[...]).astype(o_ref.dtype)


def _rmsnorm(x, g, tm):
    m, d = x.shape
    return pl.pallas_call(
        _rmsnorm_kernel,
        grid=(m // tm,),
        in_specs=[pl.BlockSpec((tm, d), lambda i: (i, 0)), pl.BlockSpec((1, d), lambda i: (0, 0))],
        out_specs=pl.BlockSpec((tm, d), lambda i: (i, 0)),
        out_shape=jax.ShapeDtypeStruct((m, d), BF16),
        compiler_params=_params("parallel"),
        name="rmsnorm",
    )(x, g.reshape(1, d))


def _matmul_kernel(a_ref, w_ref, o_ref):
    o_ref[...] = _mm(a_ref[...], w_ref[...]).astype(o_ref.dtype)


def _matmul(a, w, *, tm, tn, name, n=None):
    m, kd = a.shape
    n = w.shape[1] if n is None else n
    assert n % tn == 0 and n <= w.shape[1]
    return pl.pallas_call(
        _matmul_kernel,
        grid=(m // tm, n // tn),
        in_specs=[pl.BlockSpec((tm, kd), lambda i, j: (i, 0)), pl.BlockSpec((kd, tn), lambda i, j: (0, j))],
        out_specs=pl.BlockSpec((tm, tn), lambda i, j: (i, j)),
        out_shape=jax.ShapeDtypeStruct((m, n), F32),
        compiler_params=_params("parallel", "parallel"),
        name=name,
    )(a, w)


def _res_matmul_kernel(a1_ref, a2_ref, w1_ref, w2_ref, res_ref, o_ref):
    o_ref[...] = _mm(a1_ref[...], w1_ref[...]) + _mm(a2_ref[...], w2_ref[...]) + res_ref[...]


def _res_matmul(a1, a2, w, res, *, tm, tn, name):
    m, n = res.shape
    kd = a1.shape[1]
    assert a2.shape[1] == kd and w.shape[0] == 2 * kd and n % tn == 0
    a_spec = pl.BlockSpec((tm, kd), lambda i, j: (i, 0))
    tile = pl.BlockSpec((tm, tn), lambda i, j: (i, j))
    return pl.pallas_call(
        _res_matmul_kernel,
        grid=(m // tm, n // tn),
        in_specs=[a_spec, a_spec, pl.BlockSpec((kd, tn), lambda i, j: (0, j)),
                  pl.BlockSpec((kd, tn), lambda i, j: (1, j)), tile],
        out_specs=tile,
        out_shape=jax.ShapeDtypeStruct((m, n), F32),
        compiler_params=_params("parallel", "parallel"),
        name=name,
    )(a1, a2, w, w, res)


def _final_matmul_kernel(a_ref, w_ref, res_ref, g_ref, h_ref, *, nj, tn):
    j = pl.program_id(1)
    part = _mm(a_ref[...], w_ref[...])
    for jj in range(nj):
        @pl.when(j == jj)
        def _(jj=jj):
            cols = slice(jj * tn, (jj + 1) * tn)
            h_ref[:, cols] = part + res_ref[:, cols]

    @pl.when(j == nj - 1)
    def _():
        h_ref[...] = _rms(h_ref[...], RMS_EPS) * g_ref[...]


def _final_matmul(a, w, res, norm_g, *, tm, tn, name):
    m, n = res.shape
    kd = a.shape[1]
    assert kd == w.shape[0] and n % tn == 0
    nj = n // tn
    row = pl.BlockSpec((tm, n), lambda i, j: (i, 0))
    return pl.pallas_call(
        functools.partial(_final_matmul_kernel, nj=nj, tn=tn),
        grid=(m // tm, nj),
        in_specs=[pl.BlockSpec((tm, kd), lambda i, j: (i, 0)), pl.BlockSpec((kd, tn), lambda i, j: (0, j)),
                  row, pl.BlockSpec((1, n), lambda i, j: (0, 0))],
        out_specs=row,
        out_shape=jax.ShapeDtypeStruct((m, n), F32),
        compiler_params=_params("parallel", "arbitrary"),
        name=name,
    )(a, w, res, norm_g.reshape(1, n))


def _norm_gate_up_kernel(x_ref, g_ref, wg_ref, wu_ref, o_ref, u_scr):
    @pl.when(pl.program_id(1) == 0)
    def _():
        u_scr[...] = (_rms(x_ref[...], RMS_EPS) * g_ref[...]).astype(BF16)

    a = u_scr[...]
    gate = _mm(a, wg_ref[...])
    up = _mm(a, wu_ref[...])
    o_ref[...] = (gate * _sigmoid(gate) * up).astype(o_ref.dtype)


def _norm_gate_up(x, g, wg, wu, *, tm, tn):
    m, kd = x.shape
    n = wg.shape[1]
    return pl.pallas_call(
        _norm_gate_up_kernel,
        grid=(m // tm, n // tn),
        in_specs=[pl.BlockSpec((tm, kd), lambda i, j: (i, 0)), pl.BlockSpec((1, kd), lambda i, j: (0, 0)),
                  pl.BlockSpec((kd, tn), lambda i, j: (0, j)),
                  pl.BlockSpec((kd, tn), lambda i, j: (0, j))],
        out_specs=pl.BlockSpec((tm, tn), lambda i, j: (i, j)),
        out_shape=jax.ShapeDtypeStruct((m, n), BF16),
        scratch_shapes=[pltpu.VMEM((tm, kd), BF16)],
        compiler_params=_params("parallel", "arbitrary"),
        name="gate_up",
    )(x, g.reshape(1, kd), wg, wu)


def _shift_rows(cur, prev8, d, first):
    rolled = pltpu.roll(cur, d, axis=0)
    prolled = pltpu.roll(prev8, d, axis=0)
    if first is not None:
        prolled = jnp.where(first, 0.0, prolled)
    row = lax.broadcasted_iota(jnp.int32, prolled.shape, 0)
    top = jnp.where(row < d, prolled, rolled[0:SUBLANES])
    return jnp.concatenate([top, rolled[SUBLANES:]], axis=0)


def _ssd_kernel(z_ref, xs_ref, bc_ref, xsp_ref, bcp_ref, dt_ref, cwx_ref, cwbc_ref, cbx_ref, cbbc_ref,
                dtb_ref, alog_ref, dexp_ref, ng_ref, e_ref, tril_ref, o_ref, h_ref, y_ref, *, heads, chunks):
    first = pl.program_id(1) == 0

    @pl.when(first)
    def _():
        h_ref[...] = jnp.zeros_like(h_ref)

    for c in range(chunks):
        _ssd_chunk(c, first, z_ref, xs_ref, bc_ref, xsp_ref, bcp_ref, dt_ref, cwx_ref, cwbc_ref, cbx_ref, cbbc_ref,
                   dtb_ref, alog_ref, dexp_ref, ng_ref, e_ref, tril_ref, o_ref, h_ref, y_ref, heads)


def _ssd_chunk(c, first, z_ref, xs_ref, bc_ref, xsp_ref, bcp_ref, dt_ref, cwx_ref, cwbc_ref, cbx_ref, cbbc_ref,
               dtb_ref, alog_ref, dexp_ref, ng_ref, e_ref, tril_ref, o_ref, h_ref, y_ref, heads):
    hpg = heads // SSD_GROUPS
    gw = hpg * HEAD_DIM
    rows = slice(c * CHUNK, (c + 1) * CHUNK)

    def conv_silu(cur_ref, prev_ref, w_ref, b_ref):
        cur = cur_ref[rows, :]
        if c == 0:
            prev8, zero_if = prev_ref[...], first
        else:
            prev8, zero_if = cur_ref[c * CHUNK - SUBLANES:c * CHUNK, :], None
        acc = cur * w_ref[CONV_WIDTH - 1:CONV_WIDTH, :] + b_ref[...]
        for d in range(1, CONV_WIDTH):
            acc = acc + _shift_rows(cur, prev8, d, zero_if) * w_ref[CONV_WIDTH - 1 - d:CONV_WIDTH - d, :]
        return acc * _sigmoid(acc)

    xs = conv_silu(xs_ref, xsp_ref, cwx_ref, cbx_ref)
    bc = conv_silu(bc_ref, bcp_ref, cwbc_ref, cbbc_ref)
    xs_bf = xs.astype(BF16)

    dt = _softplus(dt_ref[rows, :] + dtb_ref[...])
    adt = dt * (-jnp.exp(alog_ref[...]))
    cs = jnp.dot(tril_ref[...], adt, precision=HIGHEST, preferred_element_type=F32)
    cs_last = cs[CHUNK - 1:CHUNK, :]
    decay_states = jnp.exp(cs_last - cs)
    chunk_decay = jnp.exp(cs_last)
    exp_in = jnp.concatenate([dt * decay_states, jnp.exp(cs), jnp.broadcast_to(chunk_decay, (BF16_ROWS, LANES))],
                             axis=0)
    exp_hi = exp_in.astype(BF16)
    exp_lo = (exp_in - exp_hi.astype(F32)).astype(BF16)
    expanded = _mm(jnp.concatenate([exp_hi, exp_lo], axis=0), e_ref[...])
    expanded = expanded[0:exp_in.shape[0]] + expanded[exp_in.shape[0]:]
    xscale = expanded[0:CHUNK]
    ecs = expanded[CHUNK:2 * CHUNK]
    cdecay = expanded[2 * CHUNK:2 * CHUNK + 1]

    cs_t = cs.T
    dt_t = dt.T
    li = lax.broadcasted_iota(jnp.int32, (CHUNK, CHUNK), 0)
    si = lax.broadcasted_iota(jnp.int32, (CHUNK, CHUNK), 1)
    causal = li >= si
    lane = lax.broadcasted_iota(jnp.int32, (CHUNK, LANES), 1)

    cbs = []
    for g in range(SSD_GROUPS):
        bm = bc[:, g * SSD_STATE:(g + 1) * SSD_STATE].astype(BF16)
        cm = bc[:, (SSD_GROUPS + g) * SSD_STATE:(SSD_GROUPS + g + 1) * SSD_STATE].astype(BF16)
        cbs.append(lax.dot_general(cm, bm, NT_DIMS, preferred_element_type=F32))
        cols = slice(g * gw, (g + 1) * gw)
        xsc = (xs[:, cols] * xscale[:, cols]).astype(BF16)
        states = lax.dot_general(bm, xsc, TN_DIMS, preferred_element_type=F32)
        h = h_ref[g]
        y_ref[rows, cols] = _mm(cm, h.astype(BF16)) * ecs[:, cols]
        h_ref[g] = h * cdecay[:, cols] + states

    for q in range(heads // 2):
        mats = []
        for e in (2 * q, 2 * q + 1):
            diff = cs[:, e:e + 1] - cs_t[e:e + 1, :]
            lmat = jnp.exp(jnp.where(causal, diff, -jnp.inf))
            mats.append(cbs[e // hpg] * lmat * dt_t[e:e + 1, :])
        lhs = jnp.concatenate(mats, axis=0).astype(BF16)
        cols = slice(q * LANES, (q + 1) * LANES)
        res = _mm(lhs, xs_bf[:, cols])
        y_ref[rows, cols] += jnp.where(lane < HEAD_DIM, res[0:CHUNK], res[CHUNK:2 * CHUNK])

    z = z_ref[rows, :]
    y = (y_ref[rows, :] + dexp_ref[...] * xs) * (z * _sigmoid(z))
    for g in range(SSD_GROUPS):
        cols = slice(g * gw, (g + 1) * gw)
        o_ref[rows, cols] = (_rms(y[:, cols], GATED_NORM_EPS) * ng_ref[:, cols]).astype(o_ref.dtype)


def _ssd_mixer(p_main, p_small, conv_w, conv_b, dt_bias, a_log, d_skip, norm_g, *, d_ssd, col_xs, col_bc, col_dt,
               tl):
    b, s, _ = p_main.shape
    heads = d_ssd // HEAD_DIM
    d_bc = 2 * SSD_GROUPS * SSD_STATE
    assert s % tl == 0 and tl % CHUNK == 0 and heads % (2 * SSD_GROUPS) == 0 and heads <= LANES
    assert col_xs % d_ssd == 0 and col_bc % d_bc == 0 and col_dt % LANES == 0
    rows8 = tl // SUBLANES

    def prev_map(cb):
        return lambda bi, i: (bi, jnp.maximum(i * rows8 - 1, 0), cb)

    def const(shape):
        return pl.BlockSpec(shape, lambda bi, i: (0,) * len(shape))

    pad = LANES - heads
    e_mat = (jnp.arange(LANES)[:, None] == (jnp.arange(d_ssd)[None, :] // HEAD_DIM)).astype(BF16)
    tril = jnp.tril(jnp.ones((CHUNK, CHUNK), F32))
    return pl.pallas_call(
        functools.partial(_ssd_kernel, heads=heads, chunks=tl // CHUNK),
        grid=(b, s // tl),
        in_specs=[
            pl.BlockSpec((None, tl, d_ssd), lambda bi, i: (bi, i, 0)),
            pl.BlockSpec((None, tl, d_ssd), lambda bi, i: (bi, i, col_xs // d_ssd)),
            pl.BlockSpec((None, tl, d_bc), lambda bi, i: (bi, i, col_bc // d_bc)),
            pl.BlockSpec((None, SUBLANES, d_ssd), prev_map(col_xs // d_ssd)),
            pl.BlockSpec((None, SUBLANES, d_bc), prev_map(col_bc // d_bc)),
            pl.BlockSpec((None, tl, LANES), lambda bi, i: (bi, i, col_dt // LANES)),
            const((CONV_WIDTH, d_ssd)), const((CONV_WIDTH, d_bc)), const((1, d_ssd)), const((1, d_bc)),
            const((1, LANES)), const((1, LANES)), const((1, d_ssd)), const((1, d_ssd)),
            const((LANES, d_ssd)), const((CHUNK, CHUNK)),
        ],
        out_specs=pl.BlockSpec((None, tl, d_ssd), lambda bi, i: (bi, i, 0)),
        out_shape=jax.ShapeDtypeStruct((b, s, d_ssd), BF16),
        scratch_shapes=[pltpu.VMEM((SSD_GROUPS, SSD_STATE, d_ssd // SSD_GROUPS), F32),
                        pltpu.VMEM((tl, d_ssd), F32)],
        compiler_params=_params("parallel", "arbitrary"),
        name="ssd_mixer",
    )(p_main, p_main, p_main, p_main, p_main, p_small,
      conv_w[:, :d_ssd], conv_w[:, d_ssd:], conv_b[None, :d_ssd], conv_b[None, d_ssd:],
      jnp.pad(dt_bias, (0, pad))[None], jnp.pad(a_log, (0, pad))[None],
      jnp.repeat(d_skip, HEAD_DIM)[None], norm_g[None], e_mat, tril)


def _split_bf16(x, terms):
    parts = []
    for _ in range(terms):
        p = x.astype(BF16)
        parts.append(p)
        x = x - p.astype(F32)
    return parts


def _mm_split(x, w_hi, w_lo):
    x_hi, x_lo = _split_bf16(x, 2)
    n = x.shape[0]
    first = _mm(jnp.concatenate([x_hi, x_lo], axis=0), w_hi)
    return first[0:n] + first[n:] + _mm(x_hi, w_lo)


def _lockstep(gens):
    results = [None] * len(gens)
    live = list(range(len(gens)))
    while live:
        for i in list(live):
            try:
                next(gens[i])
            except StopIteration as done:
                results[i] = done.value
                live.remove(i)
    return results


def _rwkv_scan_kernel(rin_ref, kin_ref, vin_ref, rp_ref, kp_ref, vp_ref, wd_ref, ad_ref, gd_ref,
                      wdp_ref, adp_ref, gdp_ref, mur_ref, muk_ref, muv_ref, muwd_ref, muad_ref, mugd_ref,
                      w0_ref, a0_ref, w2h_ref, w2l_ref, a2h_ref, a2l_ref, g2_ref,
                      kk_ref, ka_ref, rk_ref, gw_ref, gb_ref, tril_ref, bd_ref, o_ref,
                      st_ref, *, group, chunks):
    width = group * HEAD_DIM
    first = pl.program_id(2) == 0

    @pl.when(first)
    def _():
        st_ref[...] = jnp.zeros_like(st_ref)

    def token_shift(cur_ref, prev_ref, mu_ref, c):
        cur = cur_ref[c * CHUNK:(c + 1) * CHUNK, :]
        if c == 0:
            prev = _shift_rows(cur, prev_ref[...], 1, first)
        else:
            prev = _shift_rows(cur, cur_ref[c * CHUNK - SUBLANES:c * CHUNK, :], 1, None)
        return cur + (prev - cur) * mu_ref[...]

    row = lax.broadcasted_iota(jnp.int32, (CHUNK, width), 0)
    lane = lax.broadcasted_iota(jnp.int32, (CHUNK, width), 1)
    col = lane % HEAD_DIM
    lane_head = lane // HEAD_DIM
    strict = col < row
    incl = col <= row
    eye = (col == row).astype(F32)
    bd_b = bd_ref[...]
    bd_rows = lax.broadcasted_iota(jnp.int32, (width, width), 0) // HEAD_DIM
    bd_cols = lax.broadcasted_iota(jnp.int32, (width, width), 1) // HEAD_DIM
    bd_mask = bd_rows == bd_cols

    def level_mask(m):
        return ((row // (2 * m)) == (col // (2 * m))) & ((row // m) != (col // m))

    heads_per_tile = LANES // HEAD_DIM
    half = lax.broadcasted_iota(jnp.int32, (1, LANES), 1) // HEAD_DIM
    half_masks = [(half == j).astype(BF16) for j in range(heads_per_tile)]
    zero_tile = jnp.zeros((CHUNK, LANES), BF16)

    def stack(x):
        blocks = []
        for h in range(group):
            t = h // heads_per_tile
            kept = x[:, t * LANES:(t + 1) * LANES] * half_masks[h % heads_per_tile]
            blocks.append(jnp.concatenate([kept if tt == t else zero_tile for tt in range(width // LANES)], axis=1))
        return jnp.concatenate(blocks, axis=0)

    def head_sum(x):
        n = x.shape[0]
        res = _mm(jnp.concatenate(_split_bf16(x, 2), axis=0), bd_b)
        return res[0:n] + res[n:]

    def cumsum_rows(x):
        res = _mm(tril_ref[...], jnp.concatenate(_split_bf16(x, 3), axis=1))
        return res[:, 0:width] + res[:, width:2 * width] + res[:, 2 * width:]

    def local_terms(c):
        r = token_shift(rin_ref, rp_ref, mur_ref, c)
        k = token_shift(kin_ref, kp_ref, muk_ref, c)
        v = token_shift(vin_ref, vp_ref, muv_ref, c)
        dw = _mm_split(jnp.tanh(token_shift(wd_ref, wdp_ref, muwd_ref, c)), w2h_ref[...], w2l_ref[...])
        lw = -jnp.exp(-_softplus(-(w0_ref[...] + dw)) - 0.5)
        yield
        da = _mm_split(token_shift(ad_ref, adp_ref, muad_ref, c), a2h_ref[...], a2l_ref[...])
        a = _sigmoid(a0_ref[...] + da)
        gate = _mm(_sigmoid(token_shift(gd_ref, gdp_ref, mugd_ref, c)).astype(BF16), g2_ref[...])
        yield
        kk = k * kk_ref[...]
        ss = head_sum(kk * kk)
        yield
        kk = kk / jnp.maximum(jnp.sqrt(ss), 1e-12)
        k2 = k * (1.0 + (a - 1.0) * ka_ref[...])
        bonus = head_sum(r * k2 * rk_ref[...]) * v
        yield
        cl = cumsum_rows(lw)
        yield
        e_pos = jnp.exp(cl)
        e_neg = jnp.exp(-cl)
        g_last = e_pos[CHUNK - 1:CHUNK, :]
        rt = r * e_pos
        at_b = (-kk * jnp.exp(cl - lw)).astype(BF16)
        rt_b = rt.astype(BF16)
        bt_b = (kk * a * e_neg).astype(BF16)
        kt_b = (k2 * e_neg).astype(BF16)
        v_b = v.astype(BF16)

        ar = jnp.concatenate([at_b, rt_b], axis=0)
        att_b = lax.dot_general(ar, stack(bt_b), NT_DIMS, preferred_element_type=F32)
        att_k = lax.dot_general(ar, stack(kt_b), NT_DIMS, preferred_element_type=F32)
        yield
        n_ab = jnp.where(strict, att_b[0:CHUNK], 0.0)
        a_ak = jnp.where(strict, att_k[0:CHUNK], 0.0)
        a_rb = jnp.where(incl, att_b[CHUNK:], 0.0).astype(BF16)
        a_rk = jnp.where(incl, att_k[CHUNK:], 0.0)

        tinv = eye + jnp.where(level_mask(1), n_ab, 0.0)
        m = 2
        while m < CHUNK:
            off = jnp.where(level_mask(m), n_ab, 0.0).astype(BF16)
            tb = tinv.astype(BF16)
            mid = _mm(tb, stack(off))
            yield
            tinv = tinv + _mm(mid.astype(BF16), stack(tb))
            yield
            m *= 2
        tb = tinv.astype(BF16)

        xy = _mm(jnp.concatenate([a_ak, a_rk], axis=0).astype(BF16), stack(v_b))
        p_b = _mm(tb, stack(at_b)).astype(BF16)
        yield
        u0_b = _mm(tb, stack(xy[0:CHUNK].astype(BF16))).astype(BF16)
        q_b = (rt + _mm(a_rb, stack(p_b))).astype(BF16)
        mt = lax.dot_general(p_b, bt_b, TN_DIMS, preferred_element_type=F32)
        mt_b = (jnp.where(bd_mask, mt, 0.0) * g_last).astype(BF16)
        yield
        y0 = xy[CHUNK:] + _mm(a_rb, stack(u0_b))
        nfull = lax.dot_general(jnp.concatenate([u0_b, v_b], axis=0), jnp.concatenate([bt_b, kt_b], axis=0),
                                TN_DIMS, preferred_element_type=F32)
        nt = jnp.where(lane_head == 0, nfull[0:HEAD_DIM], 0.0)
        for h in range(1, group):
            nt = nt + jnp.where(lane_head == h, nfull[h * HEAD_DIM:(h + 1) * HEAD_DIM], 0.0)
        return q_b, y0, mt_b, nt * g_last, g_last, bonus, gate

    local = _lockstep([local_terms(c) for c in range(chunks)])
    st = st_ref[...]
    ys = []
    for q_b, y0, mt_b, nt, g_last, _, _ in local:
        st_b = st.astype(BF16)
        ys.append(lax.dot_general(q_b, stack(st_b), NT_DIMS, preferred_element_type=F32) + y0)
        st = st * g_last + _mm(st_b, mt_b) + nt
    st_ref[...] = st
    ycs = [y - head_sum(y) * (1.0 / HEAD_DIM) for y in ys]
    variances = [head_sum(yc * yc) * (1.0 / HEAD_DIM) for yc in ycs]
    for c, (yc, var, loc) in enumerate(zip(ycs, variances, local)):
        yn = yc * lax.rsqrt(var + GN_EPS) * gw_ref[...] + gb_ref[...]
        o_ref[c * CHUNK:(c + 1) * CHUNK, :] = ((yn + loc[5]) * loc[6]).astype(o_ref.dtype)


def _rwkv_mixer(p_main, p_small, mu, w0, w2, a0, a2, g2, k_k, k_a, r_k, gn_w, gn_b, *, col_r, tl, group):
    b, s, _ = p_main.shape
    d = w0.shape[0]
    width = group * HEAD_DIM
    chunks = tl // CHUNK
    lw_, la_, lg_ = w2.shape[0], a2.shape[0], g2.shape[0]
    lwp, lap, lgp = (_round_up(n, LANES) for n in (lw_, la_, lg_))
    assert width % LANES == 0 and d % width == 0 and col_r % width == 0
    assert s % tl == 0 and tl % CHUNK == 0
    assert lwp % lap == 0 and (lwp + lap) % lgp == 0
    rows8 = tl // SUBLANES
    ncol = d // width

    def seq(wd_, cb):
        return pl.BlockSpec((None, tl, wd_), lambda bi, p, i: (bi, i, cb(p)))

    def prev(wd_, cb):
        return pl.BlockSpec((None, SUBLANES, wd_), lambda bi, p, i: (bi, jnp.maximum(i * rows8 - 1, 0), cb(p)))

    def par(rows=1):
        return pl.BlockSpec((rows, width), lambda bi, p, i: (0, p))

    def const(shape):
        return pl.BlockSpec(shape, lambda bi, p, i: (0,) * len(shape))

    rkv_cols = [lambda p, j=j: col_r // width + j * ncol + p for j in range(3)]
    lora_cols = [lambda p: 0, lambda p: lwp // lap, lambda p: (lwp + lap) // lgp]
    lora_w = [lwp, lap, lgp]
    tril = jnp.tril(jnp.ones((CHUNK, CHUNK), BF16))
    blk = jnp.arange(width) // HEAD_DIM
    bd = (blk[:, None] == blk[None, :]).astype(BF16)
    mu_l = mu[3 * d:]
    w2h, w2l = _split_bf16(_pad_rows(w2, lwp), 2)
    a2h, a2l = _split_bf16(_pad_rows(a2, lap), 2)
    return pl.pallas_call(
        functools.partial(_rwkv_scan_kernel, group=group, chunks=chunks),
        grid=(b, ncol, s // tl),
        in_specs=([seq(width, c) for c in rkv_cols] + [prev(width, c) for c in rkv_cols]
                  + [seq(w, c) for w, c in zip(lora_w, lora_cols)] + [prev(w, c) for w, c in zip(lora_w, lora_cols)]
                  + [par()] * 3 + [const((1, w)) for w in lora_w] + [par()] * 2
                  + [par(lwp)] * 2 + [par(lap)] * 2 + [par(lgp)] + [par()] * 5
                  + [const((CHUNK, CHUNK)), const((width, width))]),
        out_specs=pl.BlockSpec((None, tl, width), lambda bi, p, i: (bi, i, p)),
        out_shape=jax.ShapeDtypeStruct((b, s, d), BF16),
        scratch_shapes=[pltpu.VMEM((HEAD_DIM, width), F32)],
        compiler_params=_params("parallel", "parallel", "arbitrary"),
        name="rwkv_scan",
    )(p_main, p_main, p_main, p_main, p_main, p_main, p_small, p_small, p_small, p_small, p_small, p_small,
      mu[None, :d], mu[None, d:2 * d], mu[None, 2 * d:3 * d],
      _pad_vec(mu_l[:lw_], lwp)[None], _pad_vec(mu_l[lw_:lw_ + la_], lap)[None], _pad_vec(mu_l[lw_ + la_:], lgp)[None],
      w0[None], a0[None], w2h, w2l, a2h, a2l, _pad_rows(g2, lgp).astype(BF16),
      k_k[None], k_a[None], r_k[None], gn_w[None], gn_b[None], tril, bd)


def _pad_cols(w, n):
    return jnp.pad(w, ((0, 0), (0, n - w.shape[1])))


def _pad_rows(w, n):
    return jnp.pad(w, ((0, n - w.shape[0]), (0, 0)))


def _pad_vec(v, n):
    return jnp.pad(v, (0, n - v.shape[0]))


def _round_up(n, m):
    return -(-n // m) * m


def _layer(h, norm1_g, w_in, conv_w, conv_b, dt_bias, a_log, d_skip, ssd_norm_g, mu, w0, w2, a0, a2, g2,
           k_k, k_a, r_k, gn_w, gn_b, w_out, norm2_g, w_gate, w_up, w_down, out_g):
    b, s, d = h.shape
    m = b * s
    d_ssd = ssd_norm_g.shape[0]
    d_rwkv = w0.shape[0]
    d_bc = 2 * SSD_GROUPS * SSD_STATE
    heads = d_ssd // HEAD_DIM
    lw_, la_, lg_ = w2.shape[0], a2.shape[0], g2.shape[0]
    lwp, lap, lgp = (_round_up(n, LANES) for n in (lw_, la_, lg_))

    o_xbc = d_ssd
    o_dt = o_xbc + d_ssd + d_bc
    o_r = o_dt + heads
    o_wd = o_r + 3 * d_rwkv
    o_ad = o_wd + lw_
    o_gd = o_ad + la_
    w_in = w_in.astype(BF16)
    w_rkv = w_in[:, o_r:o_wd]
    w_small = jnp.concatenate([_pad_cols(w_in[:, o_wd:o_ad], lwp), _pad_cols(w_in[:, o_ad:o_gd], lap),
                               _pad_cols(w_in[:, o_gd:], lgp), _pad_cols(w_in[:, o_dt:o_r], LANES)], axis=1)
    col_dt = lwp + lap + lgp

    tm = min(1024, m)
    u = _rmsnorm(h.reshape(m, d), norm1_g, tm=min(512, m))
    p_ssd = _matmul(u, w_in, n=o_dt, tm=tm, tn=1024, name="in_proj_ssd").reshape(b, s, -1)
    p_rkv = _matmul(u, w_rkv, tm=tm, tn=1024, name="in_proj_rkv").reshape(b, s, -1)
    p_small = _matmul(u, w_small, tm=tm, tn=w_small.shape[1], name="in_proj_small").reshape(b, s, -1)

    y_ssd = _ssd_mixer(p_ssd, p_small, conv_w, conv_b, dt_bias, a_log, d_skip, ssd_norm_g,
                       d_ssd=d_ssd, col_xs=o_xbc, col_bc=o_xbc + d_ssd, col_dt=col_dt, tl=min(8 * CHUNK, s))

    scan_tl = min(1024, s)
    y_rwkv = _rwkv_mixer(p_rkv, p_small, mu, w0, w2, a0, a2, g2, k_k, k_a, r_k, gn_w, gn_b,
                         col_r=0, tl=scan_tl, group=4)

    h1 = _res_matmul(y_ssd.reshape(m, d_ssd), y_rwkv.reshape(m, d_rwkv), w_out.astype(BF16), h.reshape(m, d),
                     tm=tm, tn=512, name="out_proj")
    ff = _norm_gate_up(h1, norm2_g, w_gate.astype(BF16), w_up.astype(BF16), tm=tm, tn=512)
    out = _final_matmul(ff, w_down.astype(BF16), h1, out_g, tm=min(512, m), tn=1024, name="down_proj")
    return out.reshape(b, s, d)


def kernel(x, norm1_g, w_in, ssd_conv_w, ssd_conv_b, ssd_dt_bias, ssd_A_log, ssd_D, ssd_norm_g, rwkv_mu, rwkv_w0,
           rwkv_w2, rwkv_a0, rwkv_a2, rwkv_g2, rwkv_k_k, rwkv_k_a, rwkv_r_k, rwkv_gn_w, rwkv_gn_b, w_out,
           norm2_g, w_gate, w_up, w_down, norm_f_g):
    assert w_in.shape[0] == 1, "single-layer block"
    return _layer(x, norm1_g[0], w_in[0], ssd_conv_w[0], ssd_conv_b[0], ssd_dt_bias[0], ssd_A_log[0], ssd_D[0],
                  ssd_norm_g[0], rwkv_mu[0], rwkv_w0[0], rwkv_w2[0], rwkv_a0[0], rwkv_a2[0], rwkv_g2[0],
                  rwkv_k_k[0], rwkv_k_a[0], rwkv_r_k[0], rwkv_gn_w[0], rwkv_gn_b[0], w_out[0], norm2_g[0],
                  w_gate[0], w_up[0], w_down[0], norm_f_g)
```

```python
import functools

import jax
import jax.numpy as jnp
from jax import lax
from jax.experimental import pallas as pl
from jax.experimental.pallas import tpu as pltpu

F32 = jnp.float32
BF16 = jnp.bfloat16
HIGHEST = lax.Precision.HIGHEST

HEAD_DIM = 64
CHUNK = 64
SSD_GROUPS = 4
SSD_STATE = 128
CONV_WIDTH = 4
RMS_EPS = 1e-6
GATED_NORM_EPS = 1e-5
GN_EPS = 64e-5

LANES = 128
SUBLANES = 8
BF16_ROWS = 2 * SUBLANES
VMEM_LIMIT_BYTES = 56 * 1024 * 1024

NT_DIMS = (((1,), (1,)), ((), ()))
TN_DIMS = (((0,), (0,)), ((), ()))


def _sigmoid(x):
    return jax.nn.sigmoid(x)


def _softplus(x):
    return jnp.maximum(x, 0.0) + jnp.log1p(jnp.exp(-jnp.abs(x)))


def _rms(h, eps):
    return h * lax.rsqrt(jnp.mean(h * h, axis=-1, keepdims=True) + eps)


def _params(*sem):
    return pltpu.CompilerParams(dimension_semantics=sem, vmem_limit_bytes=VMEM_LIMIT_BYTES)


def _mm(a, b):
    return jnp.dot(a, b, preferred_element_type=F32)


def _rmsnorm_kernel(x_ref, g_ref, o_ref):
    o_ref[...] = (_rms(x_ref[...], RMS_EPS) * g_ref[...]).astype(o_ref.dtype)


def _rmsnorm(x, g, tm):
    m, d = x.shape
    return pl.pallas_call(
        _rmsnorm_kernel,
        grid=(m // tm,),
        in_specs=[pl.BlockSpec((tm, d), lambda i: (i, 0)), pl.BlockSpec((1, d), lambda i: (0, 0))],
        out_specs=pl.BlockSpec((tm, d), lambda i: (i, 0)),
        out_shape=jax.ShapeDtypeStruct((m, d), BF16),
        compiler_params=_params("parallel"),
        name="rmsnorm",
    )(x, g.reshape(1, d))


def _matmul_kernel(a_ref, w_ref, o_ref):
    o_ref[...] = _mm(a_ref[...], w_ref[...]).astype(o_ref.dtype)


def _matmul(a, w, *, tm, tn, name, n=None):
    m, kd = a.shape
    n = w.shape[1] if n is None else n
    assert n % tn == 0 and n <= w.shape[1]
    return pl.pallas_call(
        _matmul_kernel,
        grid=(m // tm, n // tn),
        in_specs=[pl.BlockSpec((tm, kd), lambda i, j: (i, 0)), pl.BlockSpec((kd, tn), lambda i, j: (0, j))],
        out_specs=pl.BlockSpec((tm, tn), lambda i, j: (i, j)),
        out_shape=jax.ShapeDtypeStruct((m, n), F32),
        compiler_params=_params("parallel", "parallel"),
        name=name,
    )(a, w)


def _res_matmul_kernel(a1_ref, a2_ref, w1_ref, w2_ref, res_ref, o_ref):
    o_ref[...] = _mm(a1_ref[...], w1_ref[...]) + _mm(a2_ref[...], w2_ref[...]) + res_ref[...]


def _res_matmul(a1, a2, w, res, *, tm, tn, name):
    m, n = res.shape
    kd = a1.shape[1]
    assert a2.shape[1] == kd and w.shape[0] == 2 * kd and n % tn == 0
    a_spec = pl.BlockSpec((tm, kd), lambda i, j: (i, 0))
    tile = pl.BlockSpec((tm, tn), lambda i, j: (i, j))
    return pl.pallas_call(
        _res_matmul_kernel,
        grid=(m // tm, n // tn),
        in_specs=[a_spec, a_spec, pl.BlockSpec((kd, tn), lambda i, j: (0, j)),
                  pl.BlockSpec((kd, tn), lambda i, j: (1, j)), tile],
        out_specs=tile,
        out_shape=jax.ShapeDtypeStruct((m, n), F32),
        compiler_params=_params("parallel", "parallel"),
        name=name,
    )(a1, a2, w, w, res)


def _final_matmul_kernel(a_ref, w_ref, res_ref, g_ref, h_ref, *, nj, tn):
    j = pl.program_id(1)
    part = _mm(a_ref[...], w_ref[...])
    for jj in range(nj):
        @pl.when(j == jj)
        def _(jj=jj):
            cols = slice(jj * tn, (jj + 1) * tn)
            h_ref[:, cols] = part + res_ref[:, cols]

    @pl.when(j == nj - 1)
    def _():
        h_ref[...] = _rms(h_ref[...], RMS_EPS) * g_ref[...]


def _final_matmul(a, w, res, norm_g, *, tm, tn, name):
    m, n = res.shape
    kd = a.shape[1]
    assert kd == w.shape[0] and n % tn == 0
    nj = n // tn
    row = pl.BlockSpec((tm, n), lambda i, j: (i, 0))
    return pl.pallas_call(
        functools.partial(_final_matmul_kernel, nj=nj, tn=tn),
        grid=(m // tm, nj),
        in_specs=[pl.BlockSpec((tm, kd), lambda i, j: (i, 0)), pl.BlockSpec((kd, tn), lambda i, j: (0, j)),
                  row, pl.BlockSpec((1, n), lambda i, j: (0, 0))],
        out_specs=row,
        out_shape=jax.ShapeDtypeStruct((m, n), F32),
        compiler_params=_params("parallel", "arbitrary"),
        name=name,
    )(a, w, res, norm_g.reshape(1, n))


def _norm_gate_up_kernel(x_ref, g_ref, wg_ref, wu_ref, o_ref, u_scr):
    @pl.when(pl.program_id(1) == 0)
    def _():
        u_scr[...] = (_rms(x_ref[...], RMS_EPS) * g_ref[...]).astype(BF16)

    a = u_scr[...]
    gate = _mm(a, wg_ref[...])
    up = _mm(a, wu_ref[...])
    o_ref[...] = (gate * _sigmoid(gate) * up).astype(o_ref.dtype)


def _norm_gate_up(x, g, wg, wu, *, tm, tn):
    m, kd = x.shape
    n = wg.shape[1]
    return pl.pallas_call(
        _norm_gate_up_kernel,
        grid=(m // tm, n // tn),
        in_specs=[pl.BlockSpec((tm, kd), lambda i, j: (i, 0)), pl.BlockSpec((1, kd), lambda i, j: (0, 0)),
                  pl.BlockSpec((kd, tn), lambda i, j: (0, j)),
                  pl.BlockSpec((kd, tn), lambda i, j: (0, j))],
        out_specs=pl.BlockSpec((tm, tn), lambda i, j: (i, j)),
        out_shape=jax.ShapeDtypeStruct((m, n), BF16),
        scratch_shapes=[pltpu.VMEM((tm, kd), BF16)],
        compiler_params=_params("parallel", "arbitrary"),
        name="gate_up",
    )(x, g.reshape(1, kd), wg, wu)


def _shift_rows(cur, prev8, d, first):
    rolled = pltpu.roll(cur, d, axis=0)
    prolled = pltpu.roll(prev8, d, axis=0)
    if first is not None:
        prolled = jnp.where(first, 0.0, prolled)
    row = lax.broadcasted_iota(jnp.int32, prolled.shape, 0)
    top = jnp.where(row < d, prolled, rolled[0:SUBLANES])
    return jnp.concatenate([top, rolled[SUBLANES:]], axis=0)


def _ssd_kernel(z_ref, xs_ref, bc_ref, xsp_ref, bcp_ref, dt_ref, cwx_ref, cwbc_ref, cbx_ref, cbbc_ref,
                dtb_ref, alog_ref, dexp_ref, ng_ref, e_ref, tril_ref, o_ref, h_ref, y_ref, *, heads, chunks):
    first = pl.program_id(1) == 0

    @pl.when(first)
    def _():
        h_ref[...] = jnp.zeros_like(h_ref)

    for c in range(chunks):
        _ssd_chunk(c, first, z_ref, xs_ref, bc_ref, xsp_ref, bcp_ref, dt_ref, cwx_ref, cwbc_ref, cbx_ref, cbbc_ref,
                   dtb_ref, alog_ref, dexp_ref, ng_ref, e_ref, tril_ref, o_ref, h_ref, y_ref, heads)


def _ssd_chunk(c, first, z_ref, xs_ref, bc_ref, xsp_ref, bcp_ref, dt_ref, cwx_ref, cwbc_ref, cbx_ref, cbbc_ref,
               dtb_ref, alog_ref, dexp_ref, ng_ref, e_ref, tril_ref, o_ref, h_ref, y_ref, heads):
    hpg = heads // SSD_GROUPS
    gw = hpg * HEAD_DIM
    rows = slice(c * CHUNK, (c + 1) * CHUNK)

    def conv_silu(cur_ref, prev_ref, w_ref, b_ref):
        cur = cur_ref[rows, :]
        if c == 0:
            prev8, zero_if = prev_ref[...], first
        else:
            prev8, zero_if = cur_ref[c * CHUNK - SUBLANES:c * CHUNK, :], None
        acc = cur * w_ref[CONV_WIDTH - 1:CONV_WIDTH, :] + b_ref[...]
        for d in range(1, CONV_WIDTH):
            acc = acc + _shift_rows(cur, prev8, d, zero_if) * w_ref[CONV_WIDTH - 1 - d:CONV_WIDTH - d, :]
        return acc * _sigmoid(acc)

    xs = conv_silu(xs_ref, xsp_ref, cwx_ref, cbx_ref)
    bc = conv_silu(bc_ref, bcp_ref, cwbc_ref, cbbc_ref)
    xs_bf = xs.astype(BF16)

    dt = _softplus(dt_ref[rows, :] + dtb_ref[...])
    adt = dt * (-jnp.exp(alog_ref[...]))
    cs = jnp.dot(tril_ref[...], adt, precision=HIGHEST, preferred_element_type=F32)
    cs_last = cs[CHUNK - 1:CHUNK, :]
    decay_states = jnp.exp(cs_last - cs)
    chunk_decay = jnp.exp(cs_last)
    exp_in = jnp.concatenate([dt * decay_states, jnp.exp(cs), jnp.broadcast_to(chunk_decay, (BF16_ROWS, LANES))],
                             axis=0)
    exp_hi = exp_in.astype(BF16)
    exp_lo = (exp_in - exp_hi.astype(F32)).astype(BF16)
    expanded = _mm(jnp.concatenate([exp_hi, exp_lo], axis=0), e_ref[...])
    expanded = expanded[0:exp_in.shape[0]] + expanded[exp_in.shape[0]:]
    xscale = expanded[0:CHUNK]
    ecs = expanded[CHUNK:2 * CHUNK]
    cdecay = expanded[2 * CHUNK:2 * CHUNK + 1]

    cs_t = cs.T
    dt_t = dt.T
    li = lax.broadcasted_iota(jnp.int32, (CHUNK, CHUNK), 0)
    si = lax.broadcasted_iota(jnp.int32, (CHUNK, CHUNK), 1)
    causal = li >= si
    lane = lax.broadcasted_iota(jnp.int32, (CHUNK, LANES), 1)

    cbs = []
    for g in range(SSD_GROUPS):
        bm = bc[:, g * SSD_STATE:(g + 1) * SSD_STATE].astype(BF16)
        cm = bc[:, (SSD_GROUPS + g) * SSD_STATE:(SSD_GROUPS + g + 1) * SSD_STATE].astype(BF16)
        cbs.append(lax.dot_general(cm, bm, NT_DIMS, preferred_element_type=F32))
        cols = slice(g * gw, (g + 1) * gw)
        xsc = (xs[:, cols] * xscale[:, cols]).astype(BF16)
        states = lax.dot_general(bm, xsc, TN_DIMS, preferred_element_type=F32)
        h = h_ref[g]
        y_ref[rows, cols] = _mm(cm, h.astype(BF16)) * ecs[:, cols]
        h_ref[g] = h * cdecay[:, cols] + states

    for q in range(heads // 2):
        mats = []
        for e in (2 * q, 2 * q + 1):
            diff = cs[:, e:e + 1] - cs_t[e:e + 1, :]
            lmat = jnp.exp(jnp.where(causal, diff, -jnp.inf))
            mats.append(cbs[e // hpg] * lmat * dt_t[e:e + 1, :])
        lhs = jnp.concatenate(mats, axis=0).astype(BF16)
        cols = slice(q * LANES, (q + 1) * LANES)
        res = _mm(lhs, xs_bf[:, cols])
        y_ref[rows, cols] += jnp.where(lane < HEAD_DIM, res[0:CHUNK], res[CHUNK:2 * CHUNK])

    z = z_ref[rows, :]
    y = (y_ref[rows, :] + dexp_ref[...] * xs) * (z * _sigmoid(z))
    for g in range(SSD_GROUPS):
        cols = slice(g * gw, (g + 1) * gw)
        o_ref[rows, cols] = (_rms(y[:, cols], GATED_NORM_EPS) * ng_ref[:, cols]).astype(o_ref.dtype)


def _ssd_mixer(p_main, p_small, conv_w, conv_b, dt_bias, a_log, d_skip, norm_g, *, d_ssd, col_xs, col_bc, col_dt,
               tl):
    b, s, _ = p_main.shape
    heads = d_ssd // HEAD_DIM
    d_bc = 2 * SSD_GROUPS * SSD_STATE
    assert s % tl == 0 and tl % CHUNK == 0 and heads % (2 * SSD_GROUPS) == 0 and heads <= LANES
    assert col_xs % d_ssd == 0 and col_bc % d_bc == 0 and col_dt % LANES == 0
    rows8 = tl // SUBLANES

    def prev_map(cb):
        return lambda bi, i: (bi, jnp.maximum(i * rows8 - 1, 0), cb)

    def const(shape):
        return pl.BlockSpec(shape, lambda bi, i: (0,) * len(shape))

    pad = LANES - heads
    e_mat = (jnp.arange(LANES)[:, None] == (jnp.arange(d_ssd)[None, :] // HEAD_DIM)).astype(BF16)
    tril = jnp.tril(jnp.ones((CHUNK, CHUNK), F32))
    return pl.pallas_call(
        functools.partial(_ssd_kernel, heads=heads, chunks=tl // CHUNK),
        grid=(b, s // tl),
        in_specs=[
            pl.BlockSpec((None, tl, d_ssd), lambda bi, i: (bi, i, 0)),
            pl.BlockSpec((None, tl, d_ssd), lambda bi, i: (bi, i, col_xs // d_ssd)),
            pl.BlockSpec((None, tl, d_bc), lambda bi, i: (bi, i, col_bc // d_bc)),
            pl.BlockSpec((None, SUBLANES, d_ssd), prev_map(col_xs // d_ssd)),
            pl.BlockSpec((None, SUBLANES, d_bc), prev_map(col_bc // d_bc)),
            pl.BlockSpec((None, tl, LANES), lambda bi, i: (bi, i, col_dt // LANES)),
            const((CONV_WIDTH, d_ssd)), const((CONV_WIDTH, d_bc)), const((1, d_ssd)), const((1, d_bc)),
            const((1, LANES)), const((1, LANES)), const((1, d_ssd)), const((1, d_ssd)),
            const((LANES, d_ssd)), const((CHUNK, CHUNK)),
        ],
        out_specs=pl.BlockSpec((None, tl, d_ssd), lambda bi, i: (bi, i, 0)),
        out_shape=jax.ShapeDtypeStruct((b, s, d_ssd), BF16),
        scratch_shapes=[pltpu.VMEM((SSD_GROUPS, SSD_STATE, d_ssd // SSD_GROUPS), F32),
                        pltpu.VMEM((tl, d_ssd), F32)],
        compiler_params=_params("parallel", "arbitrary"),
        name="ssd_mixer",
    )(p_main, p_main, p_main, p_main, p_main, p_small,
      conv_w[:, :d_ssd], conv_w[:, d_ssd:], conv_b[None, :d_ssd], conv_b[None, d_ssd:],
      jnp.pad(dt_bias, (0, pad))[None], jnp.pad(a_log, (0, pad))[None],
      jnp.repeat(d_skip, HEAD_DIM)[None], norm_g[None], e_mat, tril)


def _split_bf16(x, terms):
    parts = []
    for _ in range(terms):
        p = x.astype(BF16)
        parts.append(p)
        x = x - p.astype(F32)
    return parts


def _mm_split(x, w_hi, w_lo):
    x_hi, x_lo = _split_bf16(x, 2)
    n = x.shape[0]
    first = _mm(jnp.concatenate([x_hi, x_lo], axis=0), w_hi)
    return first[0:n] + first[n:] + _mm(x_hi, w_lo)


def _lockstep(gens):
    results = [None] * len(gens)
    live = list(range(len(gens)))
    while live:
        for i in list(live):
            try:
                next(gens[i])
            except StopIteration as done:
                results[i] = done.value
                live.remove(i)
    return results


def _rwkv_scan_kernel(rin_ref, kin_ref, vin_ref, rp_ref, kp_ref, vp_ref, wd_ref, ad_ref, gd_ref,
                      wdp_ref, adp_ref, gdp_ref, mur_ref, muk_ref, muv_ref, muwd_ref, muad_ref, mugd_ref,
                      w0_ref, a0_ref, w2h_ref, w2l_ref, a2h_ref, a2l_ref, g2_ref,
                      kk_ref, ka_ref, rk_ref, gw_ref, gb_ref, tril_ref, bd_ref, o_ref,
                      st_ref, *, group, chunks):
    width = group * HEAD_DIM
    first = pl.program_id(2) == 0

    @pl.when(first)
    def _():
        st_ref[...] = jnp.zeros_like(st_ref)

    def token_shift(cur_ref, prev_ref, mu_ref, c):
        cur = cur_ref[c * CHUNK:(c + 1) * CHUNK, :]
        if c == 0:
            prev = _shift_rows(cur, prev_ref[...], 1, first)
        else:
            prev = _shift_rows(cur, cur_ref[c * CHUNK - SUBLANES:c * CHUNK, :], 1, None)
        return cur + (prev - cur) * mu_ref[...]

    row = lax.broadcasted_iota(jnp.int32, (CHUNK, width), 0)
    lane = lax.broadcasted_iota(jnp.int32, (CHUNK, width), 1)
    col = lane % HEAD_DIM
    lane_head = lane // HEAD_DIM
    strict = col < row
    incl = col <= row
    eye = (col == row).astype(F32)
    bd_b = bd_ref[...]
    bd_rows = lax.broadcasted_iota(jnp.int32, (width, width), 0) // HEAD_DIM
    bd_cols = lax.broadcasted_iota(jnp.int32, (width, width), 1) // HEAD_DIM
    bd_mask = bd_rows == bd_cols

    def level_mask(m):
        return ((row // (2 * m)) == (col // (2 * m))) & ((row // m) != (col // m))

    heads_per_tile = LANES // HEAD_DIM
    half = lax.broadcasted_iota(jnp.int32, (1, LANES), 1) // HEAD_DIM
    half_masks = [(half == j).astype(BF16) for j in range(heads_per_tile)]
    zero_tile = jnp.zeros((CHUNK, LANES), BF16)

    def stack(x):
        blocks = []
        for h in range(group):
            t = h // heads_per_tile
            kept = x[:, t * LANES:(t + 1) * LANES] * half_masks[h % heads_per_tile]
            blocks.append(jnp.concatenate([kept if tt == t else zero_tile for tt in range(width // LANES)], axis=1))
        return jnp.concatenate(blocks, axis=0)

    def head_sum(x):
        n = x.shape[0]
        res = _mm(jnp.concatenate(_split_bf16(x, 2), axis=0), bd_b)
        return res[0:n] + res[n:]

    def cumsum_rows(x):
        res = _mm(tril_ref[...], jnp.concatenate(_split_bf16(x, 3), axis=1))
        return res[:, 0:width] + res[:, width:2 * width] + res[:, 2 * width:]

    def local_terms(c):
        r = token_shift(rin_ref, rp_ref, mur_ref, c)
        k = token_shift(kin_ref, kp_ref, muk_ref, c)
        v = token_shift(vin_ref, vp_ref, muv_ref, c)
        dw = _mm_split(jnp.tanh(token_shift(wd_ref, wdp_ref, muwd_ref, c)), w2h_ref[...], w2l_ref[...])
        lw = -jnp.exp(-_softplus(-(w0_ref[...] + dw)) - 0.5)
        yield
        da = _mm_split(token_shift(ad_ref, adp_ref, muad_ref, c), a2h_ref[...], a2l_ref[...])
        a = _sigmoid(a0_ref[...] + da)
        gate = _mm(_sigmoid(token_shift(gd_ref, gdp_ref, mugd_ref, c)).astype(BF16), g2_ref[...])
        yield
        kk = k * kk_ref[...]
        ss = head_sum(kk * kk)
        yield
        kk = kk / jnp.maximum(jnp.sqrt(ss), 1e-12)
        k2 = k * (1.0 + (a - 1.0) * ka_ref[...])
        bonus = head_sum(r * k2 * rk_ref[...]) * v
        yield
        cl = cumsum_rows(lw)
        yield
        e_pos = jnp.exp(cl)
        e_neg = jnp.exp(-cl)
        g_last = e_pos[CHUNK - 1:CHUNK, :]
        rt = r * e_pos
        at_b = (-kk * jnp.exp(cl - lw)).astype(BF16)
        rt_b = rt.astype(BF16)
        bt_b = (kk * a * e_neg).astype(BF16)
        kt_b = (k2 * e_neg).astype(BF16)
        v_b = v.astype(BF16)

        ar = jnp.concatenate([at_b, rt_b], axis=0)
        att_b = lax.dot_general(ar, stack(bt_b), NT_DIMS, preferred_element_type=F32)
        att_k = lax.dot_general(ar, stack(kt_b), NT_DIMS, preferred_element_type=F32)
        yield
        n_ab = jnp.where(strict, att_b[0:CHUNK], 0.0)
        a_ak = jnp.where(strict, att_k[0:CHUNK], 0.0)
        a_rb = jnp.where(incl, att_b[CHUNK:], 0.0).astype(BF16)
        a_rk = jnp.where(incl, att_k[CHUNK:], 0.0)

        tinv = eye + jnp.where(level_mask(1), n_ab, 0.0)
        m = 2
        while m < CHUNK:
            off = jnp.where(level_mask(m), n_ab, 0.0).astype(BF16)
            tb = tinv.astype(BF16)
            mid = _mm(tb, stack(off))
            yield
            tinv = tinv + _mm(mid.astype(BF16), stack(tb))
            yield
            m *= 2
        tb = tinv.astype(BF16)

        xy = _mm(jnp.concatenate([a_ak, a_rk], axis=0).astype(BF16), stack(v_b))
        p_b = _mm(tb, stack(at_b)).astype(BF16)
        yield
        u0_b = _mm(tb, stack(xy[0:CHUNK].astype(BF16))).astype(BF16)
        q_b = (rt + _mm(a_rb, stack(p_b))).astype(BF16)
        mt = lax.dot_general(p_b, bt_b, TN_DIMS, preferred_element_type=F32)
        mt_b = (jnp.where(bd_mask, mt, 0.0) * g_last).astype(BF16)
        yield
        y0 = xy[CHUNK:] + _mm(a_rb, stack(u0_b))
        nfull = lax.dot_general(jnp.concatenate([u0_b, v_b], axis=0), jnp.concatenate([bt_b, kt_b], axis=0),
                                TN_DIMS, preferred_element_type=F32)
        nt = jnp.where(lane_head == 0, nfull[0:HEAD_DIM], 0.0)
        for h in range(1, group):
            nt = nt + jnp.where(lane_head == h, nfull[h * HEAD_DIM:(h + 1) * HEAD_DIM], 0.0)
        return q_b, y0, mt_b, nt * g_last, g_last, bonus, gate

    local = _lockstep([local_terms(c) for c in range(chunks)])
    st = st_ref[...]
    ys = []
    for q_b, y0, mt_b, nt, g_last, _, _ in local:
        st_b = st.astype(BF16)
        ys.append(lax.dot_general(q_b, stack(st_b), NT_DIMS, preferred_element_type=F32) + y0)
        st = st * g_last + _mm(st_b, mt_b) + nt
    st_ref[...] = st
    ycs = [y - head_sum(y) * (1.0 / HEAD_DIM) for y in ys]
    variances = [head_sum(yc * yc) * (1.0 / HEAD_DIM) for yc in ycs]
    for c, (yc, var, loc) in enumerate(zip(ycs, variances, local)):
        yn = yc * lax.rsqrt(var + GN_EPS) * gw_ref[...] + gb_ref[...]
        o_ref[c * CHUNK:(c + 1) * CHUNK, :] = ((yn + loc[5]) * loc[6]).astype(o_ref.dtype)


def _rwkv_mixer(p_main, p_small, mu, w0, w2, a0, a2, g2, k_k, k_a, r_k, gn_w, gn_b, *, col_r, tl, group):
    b, s, _ = p_main.shape
    d = w0.shape[0]
    width = group * HEAD_DIM
    chunks = tl // CHUNK
    lw_, la_, lg_ = w2.shape[0], a2.shape[0], g2.shape[0]
    lwp, lap, lgp = (_round_up(n, LANES) for n in (lw_, la_, lg_))
    assert width % LANES == 0 and d % width == 0 and col_r % width == 0
    assert s % tl == 0 and tl % CHUNK == 0
    assert lwp % lap == 0 and (lwp + lap) % lgp == 0
    rows8 = tl // SUBLANES
    ncol = d // width

    def seq(wd_, cb):
        return pl.BlockSpec((None, tl, wd_), lambda bi, p, i: (bi, i, cb(p)))

    def prev(wd_, cb):
        return pl.BlockSpec((None, SUBLANES, wd_), lambda bi, p, i: (bi, jnp.maximum(i * rows8 - 1, 0), cb(p)))

    def par(rows=1):
        return pl.BlockSpec((rows, width), lambda bi, p, i: (0, p))

    def const(shape):
        return pl.BlockSpec(shape, lambda bi, p, i: (0,) * len(shape))

    rkv_cols = [lambda p, j=j: col_r // width + j * ncol + p for j in range(3)]
    lora_cols = [lambda p: 0, lambda p: lwp // lap, lambda p: (lwp + lap) // lgp]
    lora_w = [lwp, lap, lgp]
    tril = jnp.tril(jnp.ones((CHUNK, CHUNK), BF16))
    blk = jnp.arange(width) // HEAD_DIM
    bd = (blk[:, None] == blk[None, :]).astype(BF16)
    mu_l = mu[3 * d:]
    w2h, w2l = _split_bf16(_pad_rows(w2, lwp), 2)
    a2h, a2l = _split_bf16(_pad_rows(a2, lap), 2)
    return pl.pallas_call(
        functools.partial(_rwkv_scan_kernel, group=group, chunks=chunks),
        grid=(b, ncol, s // tl),
        in_specs=([seq(width, c) for c in rkv_cols] + [prev(width, c) for c in rkv_cols]
                  + [seq(w, c) for w, c in zip(lora_w, lora_cols)] + [prev(w, c) for w, c in zip(lora_w, lora_cols)]
                  + [par()] * 3 + [const((1, w)) for w in lora_w] + [par()] * 2
                  + [par(lwp)] * 2 + [par(lap)] * 2 + [par(lgp)] + [par()] * 5
                  + [const((CHUNK, CHUNK)), const((width, width))]),
        out_specs=pl.BlockSpec((None, tl, width), lambda bi, p, i: (bi, i, p)),
        out_shape=jax.ShapeDtypeStruct((b, s, d), BF16),
        scratch_shapes=[pltpu.VMEM((HEAD_DIM, width), F32)],
        compiler_params=_params("parallel", "parallel", "arbitrary"),
        name="rwkv_scan",
    )(p_main, p_main, p_main, p_main, p_main, p_main, p_small, p_small, p_small, p_small, p_small, p_small,
      mu[None, :d], mu[None, d:2 * d], mu[None, 2 * d:3 * d],
      _pad_vec(mu_l[:lw_], lwp)[None], _pad_vec(mu_l[lw_:lw_ + la_], lap)[None], _pad_vec(mu_l[lw_ + la_:], lgp)[None],
      w0[None], a0[None], w2h, w2l, a2h, a2l, _pad_rows(g2, lgp).astype(BF16),
      k_k[None], k_a[None], r_k[None], gn_w[None], gn_b[None], tril, bd)


def _pad_cols(w, n):
    return jnp.pad(w, ((0, 0), (0, n - w.shape[1])))


def _pad_rows(w, n):
    return jnp.pad(w, ((0, n - w.shape[0]), (0, 0)))


def _pad_vec(v, n):
    return jnp.pad(v, (0, n - v.shape[0]))


def _round_up(n, m):
    return -(-n // m) * m


def _layer(h, norm1_g, w_in, conv_w, conv_b, dt_bias, a_log, d_skip, ssd_norm_g, mu, w0, w2, a0, a2, g2,
           k_k, k_a, r_k, gn_w, gn_b, w_out, norm2_g, w_gate, w_up, w_down, out_g):
    b, s, d = h.shape
    m = b * s
    d_ssd = ssd_norm_g.shape[0]
    d_rwkv = w0.shape[0]
    d_bc = 2 * SSD_GROUPS * SSD_STATE
    heads = d_ssd // HEAD_DIM
    lw_, la_, lg_ = w2.shape[0], a2.shape[0], g2.shape[0]
    lwp, lap, lgp = (_round_up(n, LANES) for n in (lw_, la_, lg_))

    o_xbc = d_ssd
    o_dt = o_xbc + d_ssd + d_bc
    o_r = o_dt + heads
    o_wd = o_r + 3 * d_rwkv
    o_ad = o_wd + lw_
    o_gd = o_ad + la_
    w_in = w_in.astype(BF16)
    w_rkv = w_in[:, o_r:o_wd]
    w_small = jnp.concatenate([_pad_cols(w_in[:, o_wd:o_ad], lwp), _pad_cols(w_in[:, o_ad:o_gd], lap),
                               _pad_cols(w_in[:, o_gd:], lgp), _pad_cols(w_in[:, o_dt:o_r], LANES)], axis=1)
    col_dt = lwp + lap + lgp

    tm = min(1024, m)
    u = _rmsnorm(h.reshape(m, d), norm1_g, tm=min(512, m))
    p_ssd = _matmul(u, w_in, n=o_dt, tm=tm, tn=1024, name="in_proj_ssd").reshape(b, s, -1)
    p_rkv = _matmul(u, w_rkv, tm=tm, tn=1024, name="in_proj_rkv").reshape(b, s, -1)
    p_small = _matmul(u, w_small, tm=tm, tn=w_small.shape[1], name="in_proj_small").reshape(b, s, -1)

    y_ssd = _ssd_mixer(p_ssd, p_small, conv_w, conv_b, dt_bias, a_log, d_skip, ssd_norm_g,
                       d_ssd=d_ssd, col_xs=o_xbc, col_bc=o_xbc + d_ssd, col_dt=col_dt, tl=min(8 * CHUNK, s))

    scan_tl = min(1024, s)
    y_rwkv = _rwkv_mixer(p_rkv, p_small, mu, w0, w2, a0, a2, g2, k_k, k_a, r_k, gn_w, gn_b,
                         col_r=0, tl=scan_tl, group=4)

    h1 = _res_matmul(y_ssd.reshape(m, d_ssd), y_rwkv.reshape(m, d_rwkv), w_out.astype(BF16), h.reshape(m, d),
                     tm=tm, tn=1024, name="out_proj")
    ff = _norm_gate_up(h1, norm2_g, w_gate.astype(BF16), w_up.astype(BF16), tm=tm, tn=512)
    out = _final_matmul(ff, w_down.astype(BF16), h1, out_g, tm=min(512, m), tn=1024, name="down_proj")
    return out.reshape(b, s, d)


def kernel(x, norm1_g, w_in, ssd_conv_w, ssd_conv_b, ssd_dt_bias, ssd_A_log, ssd_D, ssd_norm_g, rwkv_mu, rwkv_w0,
           rwkv_w2, rwkv_a0, rwkv_a2, rwkv_g2, rwkv_k_k, rwkv_k_a, rwkv_r_k, rwkv_gn_w, rwkv_gn_b, w_out,
           norm2_g, w_gate, w_up, w_down, norm_f_g):
    assert w_in.shape[0] == 1, "single-layer block"
    return _layer(x, norm1_g[0], w_in[0], ssd_conv_w[0], ssd_conv_b[0], ssd_dt_bias[0], ssd_A_log[0], ssd_D[0],
                  ssd_norm_g[0], rwkv_mu[0], rwkv_w0[0], rwkv_w2[0], rwkv_a0[0], rwkv_a2[0], rwkv_g2[0],
                  rwkv_k_k[0], rwkv_k_a[0], rwkv_r_k[0], rwkv_gn_w[0], rwkv_gn_b[0], w_out[0], norm2_g[0],
                  w_gate[0], w_up[0], w_down[0], norm_f_g)
```
